```python
import math
import jax, jax.numpy as jnp
from jax import lax
import numpy as np

D_MODEL = 1024
BATCH = 16
SEQ = 2048
DEPTH = 4
DEC_BATCH = 8
DEC_SEQ = 64
PAST_LEN = 1024

CHUNK = 64
D_LRU = 512
N_LRU_HEADS = 8
LRU_HEAD_DIM = D_LRU // N_LRU_HEADS
CONV_WIDTH = 4
LRU_C = 8.0
D_S5 = 512
S5_GROUP = 16
N_S5_GROUPS = D_S5 // S5_GROUP
S5_STATE = 64
D_MIX = D_LRU + D_S5
D_IN = 2 * D_LRU + D_S5
PEER_HEADS = 8
PEER_NKEYS = 128
PEER_EXPERTS = PEER_NKEYS * PEER_NKEYS
PEER_DQ = 256
PEER_TOPK = 16
PEER_BLOCK = 256
EPS = 1e-6

kernel_name = "hybrid_rglru_s5_peer_stream_step"


def rmsnorm(x, g):
    xf = x.astype(jnp.float32)
    y = xf * lax.rsqrt(jnp.mean(xf * xf, axis=-1, keepdims=True) + EPS)
    return (y * g.astype(jnp.float32)).astype(x.dtype)


def causal_conv(xa, buf, w, b):
    t = xa.shape[1]
    xp = jnp.concatenate([buf.astype(xa.dtype), xa], axis=1)
    y = b + sum(xp[:, k:k + t] * w[k] for k in range(CONV_WIDTH))
    return y, xp[:, xp.shape[1] - (CONV_WIDTH - 1):]


def _lin_combine(left, right):
    a_l, b_l = left
    a_r, b_r = right
    return a_l * a_r, a_r * b_l + b_r


def _cplx_combine(left, right):
    alr, ali, blr, bli = left
    arr, ari, brr, bri = right
    return (arr * alr - ari * ali, arr * ali + ari * alr,
            arr * blr - ari * bli + brr, arr * bli + ari * blr + bri)


def rglru(xc, h0, w_a, b_a, w_x, b_x, lam):
    f32 = jnp.float32
    bsz, t, _ = xc.shape
    xf = xc.astype(f32)
    xh = xf.reshape(bsz, t, N_LRU_HEADS, LRU_HEAD_DIM)
    r = jax.nn.sigmoid(jnp.einsum("bthi,hij->bthj", xh, w_a.astype(f32)).reshape(bsz, t, D_LRU) + b_a.astype(f32))
    i = jax.nn.sigmoid(jnp.einsum("bthi,hij->bthj", xh, w_x.astype(f32)).reshape(bsz, t, D_LRU) + b_x.astype(f32))
    log_a = -LRU_C * r * jax.nn.softplus(-lam.astype(f32))
    a = jnp.exp(log_a)
    bt = jnp.sqrt(-jnp.expm1(2.0 * log_a)) * (i * xf)
    bt = bt.at[:, 0].add(a[:, 0] * h0.astype(f32))
    _, h = lax.associative_scan(_lin_combine, (a, bt), axis=1)
    return h, h[:, -1]


def s5_scan(u, s0_re, s0_im, a_re, a_im, log_dt, b_re, b_im, c_re, c_im, d):
    f32 = jnp.float32
    bsz, t, _ = u.shape
    ug = u.astype(f32).reshape(bsz, t, N_S5_GROUPS, S5_GROUP)
    lr, li = a_re.astype(f32), a_im.astype(f32)
    dt = jnp.exp(log_dt.astype(f32))[:, None]
    mag = jnp.exp(lr * dt)
    ab_re, ab_im = mag * jnp.cos(li * dt), mag * jnp.sin(li * dt)
    den = lr * lr + li * li
    nr = ab_re - 1.0
    f_re = (nr * lr + ab_im * li) / den
    f_im = (ab_im * lr - nr * li) / den
    br, bi = b_re.astype(f32), b_im.astype(f32)
    bb_re = f_re[..., None] * br - f_im[..., None] * bi
    bb_im = f_re[..., None] * bi + f_im[..., None] * br
    bu_re = jnp.einsum("btgi,gpi->btgp", ug, bb_re)
    bu_im = jnp.einsum("btgi,gpi->btgp", ug, bb_im)
    s0r, s0i = s0_re.astype(f32), s0_im.astype(f32)
    bu_re = bu_re.at[:, 0].add(ab_re * s0r - ab_im * s0i)
    bu_im = bu_im.at[:, 0].add(ab_re * s0i + ab_im * s0r)
    a_re_t = jnp.broadcast_to(ab_re, bu_re.shape)
    a_im_t = jnp.broadcast_to(ab_im, bu_re.shape)
    _, _, s_re, s_im = lax.associative_scan(_cplx_combine, (a_re_t, a_im_t, bu_re, bu_im), axis=1)
    y = (jnp.einsum("btgp,gip->btgi", s_re, c_re.astype(f32))
         - jnp.einsum("btgp,gip->btgi", s_im, c_im.astype(f32))
         + d.astype(f32).reshape(N_S5_GROUPS, S5_GROUP) * ug)
    return y.reshape(bsz, t, D_S5), s_re[:, -1], s_im[:, -1]


def peer(h, w_q, sub_keys, u_tab, v_tab):
    bsz, t, dm = h.shape
    n = bsz * t
    n_pad = -(-n // PEER_BLOCK) * PEER_BLOCK
    hf = jnp.pad(h.reshape(n, dm), ((0, n_pad - n), (0, 0)))
    blocks = hf.reshape(n_pad // PEER_BLOCK, PEER_BLOCK, dm)

    def one_block(xb):
        q = (xb @ w_q).reshape(PEER_BLOCK, PEER_HEADS, 2, PEER_DQ // 2)
        s = jnp.einsum("nhcd,hckd->nhck", q, sub_keys).astype(jnp.float32)
        v1, i1 = lax.top_k(s[:, :, 0], PEER_TOPK)
        v2, i2 = lax.top_k(s[:, :, 1], PEER_TOPK)
        cand = (v1[..., :, None] + v2[..., None, :]).reshape(PEER_BLOCK, PEER_HEADS, PEER_TOPK * PEER_TOPK)
        cidx = (i1[..., :, None] * PEER_NKEYS + i2[..., None, :]).reshape(PEER_BLOCK, PEER_HEADS, PEER_TOPK * PEER_TOPK)
        top_s, pos = lax.top_k(cand, PEER_TOPK)
        eidx = jnp.take_along_axis(cidx, pos, axis=-1)
        gate = jax.nn.softmax(top_s, axis=-1)
        act = jax.nn.gelu(jnp.einsum("nd,nhkd->nhk", xb, u_tab[eidx]).astype(jnp.float32))
        coef = (gate * act).astype(xb.dtype)
        return jnp.einsum("nhk,nhkd->nd", coef, v_tab[eidx])

    out = lax.map(one_block, blocks).reshape(n_pad, dm)[:n]
    return out.reshape(bsz, t, dm).astype(h.dtype)


def trunk_layer(x, c, conv_buf, h0, s0_re, s0_im,
                w_ada, b_ada, norm1, norm2, w_in, conv_w, conv_b,
                lru_wa, lru_ba, lru_wx, lru_bx, lru_lam,
                s5_a_re, s5_a_im, s5_log_dt, s5_b_re, s5_b_im, s5_c_re, s5_c_im, s5_d,
                s5_w_glu, s5_b_glu, gn_lru, gn_s5, w_out,
                peer_wq, peer_keys, peer_u, peer_v):
    mod = (jax.nn.silu(c) @ w_ada + b_ada)[:, None, :]
    sh1, sc1, g1, sh2, sc2, g2 = jnp.split(mod, 6, axis=-1)
    h = rmsnorm(x, norm1) * (1.0 + sc1) + sh1
    z = h @ w_in
    xa = z[..., :D_LRU]
    ga = z[..., D_LRU:2 * D_LRU]
    us = z[..., 2 * D_LRU:]
    xc, new_conv = causal_conv(xa, conv_buf, conv_w, conv_b)
    hl, h_last = rglru(xc, h0, lru_wa, lru_ba, lru_wx, lru_bx, lru_lam)
    y_lru = (hl * jax.nn.gelu(ga.astype(jnp.float32))).astype(x.dtype)
    ys, s_re, s_im = s5_scan(us, s0_re, s0_im, s5_a_re, s5_a_im, s5_log_dt,
                             s5_b_re, s5_b_im, s5_c_re, s5_c_im, s5_d)
    gy = jax.nn.gelu(ys).astype(x.dtype)
    y_s5 = gy * jax.nn.sigmoid(gy @ s5_w_glu + s5_b_glu)
    mix = jnp.concatenate([rmsnorm(y_lru, gn_lru), rmsnorm(y_s5, gn_s5)], axis=-1) @ w_out
    x = x + g1 * mix
    h = rmsnorm(x, norm2) * (1.0 + sc2) + sh2
    x = x + g2 * peer(h, peer_wq, peer_keys, peer_u, peer_v)
    return x, new_conv, h_last, s_re, s_im


def setup_inputs(seed: int = 0) -> dict:
    key = jax.random.key(seed)
    ks = iter(jax.random.split(key, 48))
    f32 = jnp.float32
    d, L, G, P = D_MODEL, DEPTH, N_S5_GROUPS, S5_STATE

    def nrm(shape, scale):
        return jax.random.normal(next(ks), shape, f32) * scale

    x_prompt = nrm((BATCH, SEQ, d), 1.0)
    x_sample = nrm((DEC_BATCH, DEC_SEQ, d), 1.0)
    cache_conv = nrm((L, DEC_BATCH, CONV_WIDTH - 1, D_LRU), 1.0)
    state_lru = nrm((L, DEC_BATCH, D_LRU), 0.5)
    state_s5_re = nrm((L, DEC_BATCH, G, P), 0.1)
    state_s5_im = nrm((L, DEC_BATCH, G, P), 0.1)
    c_prompt = nrm((BATCH, d), 1.0)
    c_sample = nrm((DEC_BATCH, d), 1.0)
    w_ada = nrm((L, d, 6 * d), 0.5 * d ** -0.5)
    b_ada = nrm((L, 6 * d), 0.02)
    norm1 = 1.0 + nrm((L, d), 0.02)
    norm2 = 1.0 + nrm((L, d), 0.02)
    w_in = nrm((L, d, D_IN), d ** -0.5)
    conv_w = nrm((L, CONV_WIDTH, D_LRU), CONV_WIDTH ** -0.5)
    conv_b = nrm((L, D_LRU), 0.02)
    lru_wa = nrm((L, N_LRU_HEADS, LRU_HEAD_DIM, LRU_HEAD_DIM), LRU_HEAD_DIM ** -0.5)
    lru_ba = nrm((L, D_LRU), 0.02)
    lru_wx = nrm((L, N_LRU_HEADS, LRU_HEAD_DIM, LRU_HEAD_DIM), LRU_HEAD_DIM ** -0.5)
    lru_bx = nrm((L, D_LRU), 0.02)
    a_c = jax.random.uniform(next(ks), (L, D_LRU), f32, 0.9, 0.999)
    s_lam = a_c ** (1.0 / LRU_C)
    lru_lam = jnp.log(s_lam) - jnp.log1p(-s_lam)
    s5_a_re = -0.5 + nrm((L, G, P), 0.01)
    s5_a_im = jnp.pi * jnp.arange(P, dtype=f32) + nrm((L, G, P), 0.01)
    s5_log_dt = jax.random.uniform(next(ks), (L, G), f32, math.log(1e-3), math.log(1e-1))
    s5_b_re = nrm((L, G, P, S5_GROUP), (2 * S5_GROUP) ** -0.5)
    s5_b_im = nrm((L, G, P, S5_GROUP), (2 * S5_GROUP) ** -0.5)
    s5_c_re = nrm((L, G, S5_GROUP, P), P ** -0.5)
    s5_c_im = nrm((L, G, S5_GROUP, P), P ** -0.5)
    s5_d = nrm((L, D_S5), 1.0)
    s5_w_glu = nrm((L, D_S5, D_S5), D_S5 ** -0.5)
    s5_b_glu = nrm((L, D_S5), 0.02)
    gn_lru = 1.0 + nrm((L, D_LRU), 0.02)
    gn_s5 = 1.0 + nrm((L, D_S5), 0.02)
    w_out = nrm((L, D_MIX, d), D_MIX ** -0.5)
    peer_wq = nrm((L, d, PEER_HEADS * PEER_DQ), d ** -0.5)
    peer_keys = nrm((L, PEER_HEADS, 2, PEER_NKEYS, PEER_DQ // 2), (PEER_DQ // 2) ** -0.5)
    peer_u = nrm((L, PEER_EXPERTS, d), d ** -0.5)
    peer_v = nrm((L, PEER_EXPERTS, d), PEER_HEADS ** -0.5)
    norm_f = 1.0 + nrm((d,), 0.02)
    return {"x_prompt": x_prompt, "x_sample": x_sample,
            "cache_conv": cache_conv, "state_lru": state_lru,
            "state_s5_re": state_s5_re, "state_s5_im": state_s5_im,
            "c_prompt": c_prompt, "c_sample": c_sample,
            "w_ada": w_ada, "b_ada": b_ada, "norm1": norm1, "norm2": norm2,
            "w_in": w_in, "conv_w": conv_w, "conv_b": conv_b,
            "lru_wa": lru_wa, "lru_ba": lru_ba, "lru_wx": lru_wx, "lru_bx": lru_bx, "lru_lam": lru_lam,
            "s5_a_re": s5_a_re, "s5_a_im": s5_a_im, "s5_log_dt": s5_log_dt,
            "s5_b_re": s5_b_re, "s5_b_im": s5_b_im, "s5_c_re": s5_c_re, "s5_c_im": s5_c_im,
            "s5_d": s5_d, "s5_w_glu": s5_w_glu, "s5_b_glu": s5_b_glu,
            "gn_lru": gn_lru, "gn_s5": gn_s5, "w_out": w_out,
            "peer_wq": peer_wq, "peer_keys": peer_keys, "peer_u": peer_u, "peer_v": peer_v,
            "norm_f": norm_f}


def reference(x_prompt, x_sample, cache_conv, state_lru, state_s5_re, state_s5_im,
              c_prompt, c_sample,
              w_ada, b_ada, norm1, norm2, w_in, conv_w, conv_b,
              lru_wa, lru_ba, lru_wx, lru_bx, lru_lam,
              s5_a_re, s5_a_im, s5_log_dt, s5_b_re, s5_b_im, s5_c_re, s5_c_im, s5_d,
              s5_w_glu, s5_b_glu, gn_lru, gn_s5, w_out,
              peer_wq, peer_keys, peer_u, peer_v, norm_f):
    layer_params = (w_ada, b_ada, norm1, norm2, w_in, conv_w, conv_b,
                    lru_wa, lru_ba, lru_wx, lru_bx, lru_lam,
                    s5_a_re, s5_a_im, s5_log_dt, s5_b_re, s5_b_im, s5_c_re, s5_c_im, s5_d,
                    s5_w_glu, s5_b_glu, gn_lru, gn_s5, w_out,
                    peer_wq, peer_keys, peer_u, peer_v)

    def run(x, c, conv, hl, sr, si):
        convs, hs, srs, sis = [], [], [], []
        for l in range(DEPTH):
            x, nc, nh, nsr, nsi = trunk_layer(x, c, conv[l], hl[l], sr[l], si[l],
                                              *[p[l] for p in layer_params])
            convs.append(nc)
            hs.append(nh)
            srs.append(nsr)
            sis.append(nsi)
        return (rmsnorm(x, norm_f), jnp.stack(convs), jnp.stack(hs),
                jnp.stack(srs), jnp.stack(sis))

    bp = x_prompt.shape[0]
    zero_conv = jnp.zeros((DEPTH, bp, CONV_WIDTH - 1, D_LRU), x_prompt.dtype)
    zero_lru = jnp.zeros((DEPTH, bp, D_LRU), jnp.float32)
    zero_s5 = jnp.zeros((DEPTH, bp, N_S5_GROUPS, S5_STATE), jnp.float32)
    y_prompt, conv_p, lru_p, s5re_p, s5im_p = run(x_prompt, c_prompt, zero_conv, zero_lru, zero_s5, zero_s5)
    y_sample, conv_s, lru_s, s5re_s, s5im_s = run(x_sample, c_sample, cache_conv, state_lru,
                                                  state_s5_re, state_s5_im)
    return (y_prompt, y_sample, conv_p, lru_p, s5re_p, s5im_p, conv_s, lru_s, s5re_s, s5im_s)
```

```python
import functools

import jax
import jax.numpy as jnp
from jax import lax
from jax.experimental import pallas as pl
from jax.experimental.pallas import tpu as pltpu

F32 = jnp.float32
BF16 = jnp.bfloat16
I32 = jnp.int32

EPS = 1e-6
LRU_C = 8.0
CONV_WIDTH = 4
N_LRU_HEADS = 8
S5_GROUP = 16
PEER_HEADS = 8
PEER_NKEYS = 128
PEER_TOPK = 16
N_PAIRS = PEER_HEADS * PEER_TOPK

LANES = 128
SUBLANES = 8
VMEM_LIMIT = 56 * 1024 * 1024

HI_MASK = -65536


def _rms(x, g):
    return x * lax.rsqrt(jnp.mean(x * x, axis=-1, keepdims=True) + EPS) * g


def _dot(a, b):
    return jnp.dot(a.astype(BF16), b, preferred_element_type=F32)


def _params(sem):
    return pltpu.CompilerParams(dimension_semantics=sem, vmem_limit_bytes=VMEM_LIMIT)


def _ada_kernel(c_ref, w_ref, b_ref, o_ref):
    c = c_ref[...]
    s = c * jax.nn.sigmoid(c)
    o_ref[...] = _dot(s, w_ref[...].astype(BF16)) + b_ref[...]


def _ada_call(c_all, w_ada, b_ada):
    nl, d, d6 = w_ada.shape
    nb = c_all.shape[0]
    bn = d
    return pl.pallas_call(
        _ada_kernel,
        grid=(nl, d6 // bn),
        in_specs=[
            pl.BlockSpec((nb, d), lambda l, j: (0, 0)),
            pl.BlockSpec((None, d, bn), lambda l, j: (l, 0, j)),
            pl.BlockSpec((None, 1, bn), lambda l, j: (l, 0, j)),
        ],
        out_specs=pl.BlockSpec((None, nb, bn), lambda l, j: (l, 0, j)),
        out_shape=jax.ShapeDtypeStruct((nl, nb, d6), F32),
        compiler_params=_params(("parallel", "parallel")),
        name="adaln_mod",
    )(c_all, w_ada, b_ada.reshape(nl, 1, d6))


def _s5_disc_kernel(are_ref, aim_ref, ldt_ref, bre_ref, bim_ref,
                    abre_ref, abim_ref, bbre_ref, bbim_ref):
    lr = are_ref[...]
    li = aim_ref[...]
    dt = jnp.exp(ldt_ref[...])
    mag = jnp.exp(lr * dt)
    ab_re = mag * jnp.cos(li * dt)
    ab_im = mag * jnp.sin(li * dt)
    den = lr * lr + li * li
    nr = ab_re - 1.0
    f_re = (nr * lr + ab_im * li) / den
    f_im = (ab_im * lr - nr * li) / den
    br = bre_ref[...]
    bi = bim_ref[...]
    abre_ref[...] = ab_re
    abim_ref[...] = ab_im
    bbre_ref[...] = f_re * br - f_im * bi
    bbim_ref[...] = f_re * bi + f_im * br


def _s5_disc_call(a_re, a_im, log_dt, bt_re, bt_im):
    nl, g, p = a_re.shape
    gi = bt_re.shape[2]
    spec_a = pl.BlockSpec((None, g, 1, p), lambda l: (l, 0, 0, 0))
    spec_b = pl.BlockSpec((None, g, gi, p), lambda l: (l, 0, 0, 0))
    return pl.pallas_call(
        _s5_disc_kernel,
        grid=(nl,),
        in_specs=[spec_a, spec_a, pl.BlockSpec((None, g, 1, 1), lambda l: (l, 0, 0, 0)), spec_b, spec_b],
        out_specs=[spec_a, spec_a, spec_b, spec_b],
        out_shape=[jax.ShapeDtypeStruct((nl, g, 1, p), F32)] * 2
        + [jax.ShapeDtypeStruct((nl, g, gi, p), F32)] * 2,
        compiler_params=_params(("parallel",)),
        name="s5_discretise",
    )(a_re.reshape(nl, g, 1, p), a_im.reshape(nl, g, 1, p), log_dt.reshape(nl, g, 1, 1), bt_re, bt_im)


def _mixer_kernel(x_ref, mod_ref, n1_ref, win_ref, cw_ref, cb_ref, wg_ref, bg_ref, lam_ref,
                  are_ref, aim_ref, bfull_ref, cre_ref, cim_ref, d_ref, wglu_ref, bglu_ref,
                  gnl_ref, gns_ref, wout_ref, conv0_ref, h0_ref, s0r_ref, s0i_ref,
                  x1_ref, convo_ref, ho_ref, sro_ref, sio_ref,
                  xa_ext, hc, scr, sci, a_scr, b_scr, s_scr):
    tc, d = x_ref.shape
    dl = hc.shape[1]
    tail = SUBLANES
    t = pl.program_id(1)

    @pl.when(t == 0)
    def _():
        xa_ext[0:tail, :] = jnp.zeros((tail, dl), F32)
        xa_ext[tail - (CONV_WIDTH - 1):tail, :] = conv0_ref[...]
        hc[...] = h0_ref[...]
        scr[...] = s0r_ref[...]
        sci[...] = s0i_ref[...]

    x = x_ref[...]
    mod = mod_ref[...]
    sh1, sc1, g1 = mod[:, 0:d], mod[:, d:2 * d], mod[:, 2 * d:3 * d]
    h = _rms(x, n1_ref[...]) * (1.0 + sc1) + sh1
    z = _dot(h, win_ref[...])
    xa = z[:, 0:dl]
    ga = z[:, dl:2 * dl]
    us = z[:, 2 * dl:]

    xa_ext[tail:tail + tc, :] = xa
    cw = cw_ref[...]
    xc = cb_ref[...] + cw[3:4, :] * xa
    for k in range(CONV_WIDTH - 1):
        back = CONV_WIDTH - 1 - k
        xc = xc + cw[k:k + 1, :] * xa_ext[tail - back:tail - back + tc, :]
    xa_ext[0:tail, :] = xa_ext[tc:tc + tail, :]
    convo_ref[...] = xa[tc - (CONV_WIDTH - 1):tc, :]

    rows = lax.broadcasted_iota(I32, (tc, LANES), 0)
    row0 = rows == 0
    ns = scr.shape[1]

    gates = _dot(xc, wg_ref[...]) + bg_ref[...]
    r = jax.nn.sigmoid(gates[:, 0:dl])
    i = jax.nn.sigmoid(gates[:, dl:])
    log_a = -LRU_C * r * jax.nn.softplus(-lam_ref[...])
    a0 = jnp.exp(log_a)
    a_scr[...] = a0
    b_scr[...] = jnp.sqrt(-jnp.tanh(log_a) * (a0 * a0 + 1.0)) * (i * xc)

    def lru_block(j, carry):
        cs = pl.ds(pl.multiple_of(j * LANES, LANES), LANES)
        a = a_scr[:, cs]
        bt = b_scr[:, cs] + jnp.where(row0, a * hc[:, cs], 0.0)
        step = 1
        while step < tc:
            m = rows >= step
            bt = jnp.where(m, a * pltpu.roll(bt, step, 0), 0.0) + bt
            if step * 2 < tc:
                a = jnp.where(m, a * pltpu.roll(a, step, 0), a)
            step *= 2
        b_scr[:, cs] = bt
        return carry

    lax.fori_loop(0, dl // LANES, lru_block, 0)
    hl = b_scr[...]
    hc[...] = hl[tc - 1:tc, :]
    ho_ref[...] = hl[tc - 1:tc, :]
    y_lru = hl * jax.nn.gelu(ga)

    s_scr[...] = _dot(us, bfull_ref[...])

    def s5_block(j, carry):
        cr = pl.ds(pl.multiple_of(j * LANES, LANES), LANES)
        ci = pl.ds(pl.multiple_of(ns + j * LANES, LANES), LANES)
        pr = are_ref[:, cr]
        pi = aim_ref[:, cr]
        c_r = scr[:, cr]
        c_i = sci[:, cr]
        sr = s_scr[:, cr] + jnp.where(row0, pr * c_r - pi * c_i, 0.0)
        si = s_scr[:, ci] + jnp.where(row0, pr * c_i + pi * c_r, 0.0)
        step = 1
        while step < tc:
            m = rows >= step
            qr = pltpu.roll(sr, step, 0)
            qi = pltpu.roll(si, step, 0)
            sr, si = (sr + jnp.where(m, pr * qr - pi * qi, 0.0),
                      si + jnp.where(m, pr * qi + pi * qr, 0.0))
            pr, pi = pr * pr - pi * pi, 2.0 * pr * pi
            step *= 2
        s_scr[:, cr] = sr
        s_scr[:, ci] = si
        return carry

    lax.fori_loop(0, ns // LANES, s5_block, 0)
    sr = s_scr[:, 0:ns]
    si = s_scr[:, ns:]
    scr[...] = sr[tc - 1:tc, :]
    sci[...] = si[tc - 1:tc, :]
    sro_ref[...] = sr[tc - 1:tc, :]
    sio_ref[...] = si[tc - 1:tc, :]
    ys = _dot(sr, cre_ref[...]) - _dot(si, cim_ref[...]) + d_ref[...] * us
    gy = jax.nn.gelu(ys)
    y_s5 = gy * jax.nn.sigmoid(_dot(gy, wglu_ref[...]) + bglu_ref[...])

    wout = wout_ref[...]
    mix = _dot(_rms(y_lru, gnl_ref[...]), wout[0:dl, :]) + _dot(_rms(y_s5, gns_ref[...]), wout[dl:, :])
    x1_ref[...] = x + g1 * mix


def _mixer_call(layer, x, mod, conv0, h0, s0r, s0i, w, tc):
    b, t, d = x.shape
    dl = h0.shape[-1]
    ns = s0r.shape[-1]

    def tok(shape):
        return pl.BlockSpec((None,) + shape, lambda bi, ti: (bi, ti, 0))

    def seq(shape):
        return pl.BlockSpec((None,) + shape, lambda bi, ti: (bi, 0, 0))

    def lay(arr):
        return pl.BlockSpec((None,) + arr.shape[1:], lambda bi, ti: (layer,) + (0,) * (arr.ndim - 1))

    names = ("norm1", "w_in", "conv_w", "conv_b", "w_gate", "b_gate", "lam", "ab_re", "ab_im",
             "b_full", "c_re", "c_im", "s5_d", "w_glu", "b_glu", "gn_lru", "gn_s5", "w_out")
    ws = [w[n] for n in names]
    return pl.pallas_call(
        _mixer_kernel,
        grid=(b, t // tc),
        in_specs=[tok((tc, d)), seq((1, mod.shape[-1]))] + [lay(a) for a in ws]
        + [seq((CONV_WIDTH - 1, dl)), seq((1, dl)), seq((1, ns)), seq((1, ns))],
        out_specs=[tok((tc, d)), seq((CONV_WIDTH - 1, dl)), seq((1, dl)), seq((1, ns)), seq((1, ns))],
        out_shape=[jax.ShapeDtypeStruct((b, t, d), F32),
                   jax.ShapeDtypeStruct((b, CONV_WIDTH - 1, dl), F32),
                   jax.ShapeDtypeStruct((b, 1, dl), F32),
                   jax.ShapeDtypeStruct((b, 1, ns), F32),
                   jax.ShapeDtypeStruct((b, 1, ns), F32)],
        scratch_shapes=[pltpu.VMEM((tc + SUBLANES, dl), F32), pltpu.VMEM((1, dl), F32),
                        pltpu.VMEM((1, ns), F32), pltpu.VMEM((1, ns), F32),
                        pltpu.VMEM((tc, dl), F32), pltpu.VMEM((tc, dl), F32),
                        pltpu.VMEM((tc, 2 * ns), F32)],
        compiler_params=_params(("parallel", "arbitrary")),
        name="mixer",
    )(x, mod, *ws, conv0, h0, s0r, s0i)


def _topk_rows(s, k, payload=None):
    nrow = s.shape[0]
    rows = lax.broadcasted_iota(I32, s.shape, 0)
    vals, outs = [], []
    for _ in range(k):
        m = jnp.max(s, axis=0, keepdims=True)
        idx = jnp.min(jnp.where(s == m, rows, nrow), axis=0, keepdims=True)
        hit = rows == idx
        vals.append(m)
        if payload is None:
            outs.append(idx)
        else:
            outs.append(jnp.max(jnp.where(hit, payload, -1), axis=0, keepdims=True))
        s = jnp.where(hit, -jnp.inf, s)
    return jnp.concatenate(vals, axis=0), jnp.concatenate(outs, axis=0)


def _select_kernel(x_ref, mod_ref, n2_ref, wq_ref, keys_ref, h2_ref, gate_ref, eidx_ref):
    tb, d = x_ref.shape
    x = x_ref[...]
    mod = mod_ref[...]
    sh2, sc2 = mod[:, 3 * d:4 * d], mod[:, 4 * d:5 * d]
    h2 = _rms(x, n2_ref[...]) * (1.0 + sc2) + sh2
    h2_ref[...] = h2
    q = _dot(h2, wq_ref[...])
    dk = keys_ref.shape[2]
    for hd in range(PEER_HEADS):
        vs, ids = [], []
        for c in range(2):
            j = 2 * hd + c
            qhc = q[:, j * dk:(j + 1) * dk].astype(BF16)
            st = lax.dot_general(keys_ref[j], qhc, (((1,), (1,)), ((), ())),
                                 preferred_element_type=F32)
            v, ix = _topk_rows(st, PEER_TOPK)
            vs.append(v)
            ids.append(ix)
        cand = jnp.concatenate([vs[0][a:a + 1, :] + vs[1] for a in range(PEER_TOPK)], axis=0)
        cidx = jnp.concatenate([ids[0][a:a + 1, :] * PEER_NKEYS + ids[1] for a in range(PEER_TOPK)], axis=0)
        top_s, eidx = _topk_rows(cand, PEER_TOPK, payload=cidx)
        e = jnp.exp(top_s - jnp.max(top_s, axis=0, keepdims=True))
        gate = e / jnp.sum(e, axis=0, keepdims=True)
        gate_ref[hd * PEER_TOPK:(hd + 1) * PEER_TOPK, :] = gate
        eidx_ref[hd * PEER_TOPK:(hd + 1) * PEER_TOPK, :] = eidx


def _select_call(layer, x1, mod, norm2, wq, keys, tb):
    b, t, d = x1.shape
    return pl.pallas_call(
        _select_kernel,
        grid=(b, t // tb),
        in_specs=[
            pl.BlockSpec((None, tb, d), lambda bi, ti: (bi, ti, 0)),
            pl.BlockSpec((None, 1, mod.shape[-1]), lambda bi, ti: (bi, 0, 0)),
            pl.BlockSpec((None, 1, d), lambda bi, ti: (layer, 0, 0)),
            pl.BlockSpec((None,) + wq.shape[1:], lambda bi, ti: (layer, 0, 0)),
            pl.BlockSpec((None,) + keys.shape[1:], lambda bi, ti: (layer, 0, 0, 0)),
        ],
        out_specs=[
            pl.BlockSpec((None, tb, d), lambda bi, ti: (bi, ti, 0)),
            pl.BlockSpec((None, N_PAIRS, tb), lambda bi, ti: (bi, 0, ti)),
            pl.BlockSpec((None, N_PAIRS, tb), lambda bi, ti: (bi, 0, ti)),
        ],
        out_shape=[jax.ShapeDtypeStruct((b, t, d), F32),
                   jax.ShapeDtypeStruct((b, N_PAIRS, t), F32),
                   jax.ShapeDtypeStruct((b, N_PAIRS, t), I32)],
        compiler_params=_params(("parallel", "parallel")),
        name="peer_select",
    )(x1, mod, norm2, wq, keys)


ROW_SLAB = 4


def _pack_table(tab):
    nl, ne, d = tab.shape
    bits = lax.bitcast_convert_type(tab.astype(BF16), jnp.uint16).astype(jnp.uint32)
    words = bits[:, :, :d // 2] | (bits[:, :, d // 2:] << 16)
    return lax.bitcast_convert_type(words, I32).reshape(nl, ne * ROW_SLAB, LANES)


def _unpack(w):
    lo = lax.bitcast_convert_type(w << 16, F32)
    hi = lax.bitcast_convert_type(w & HI_MASK, F32)
    return lo, hi


def _row_sums_8(p, row8):
    m1 = (row8 % 4) < 2
    m2 = (row8 % 2) == 0

    def pair(a, b):
        return jnp.concatenate([a, b], axis=0)

    def l1(a, b):
        ta = a + pltpu.roll(a, 6, 0)
        tb = b + pltpu.roll(b, 6, 0)
        return jnp.where(m1, ta, pltpu.roll(tb, 2, 0))

    def l2(a, b):
        ta = a + pltpu.roll(a, 7, 0)
        tb = b + pltpu.roll(b, 7, 0)
        return jnp.where(m2, ta, pltpu.roll(tb, 1, 0))

    return l2(l1(pair(p[0], p[4]), pair(p[2], p[6])), l1(pair(p[1], p[5]), pair(p[3], p[7])))


def _peer_u_kernel(idx_ref, x_ref, gate_ref, tab_ref, coef_ref, act_scr):
    npair, tu = gate_ref.shape
    row8 = lax.broadcasted_iota(I32, (SUBLANES, LANES), 0)
    lane = lax.broadcasted_iota(I32, (npair, tu), 1)
    act_scr[...] = jnp.zeros((npair, tu), F32)

    def token(t, carry):
        x8 = x_ref[pl.ds(pl.multiple_of(t * SUBLANES, SUBLANES), SUBLANES), :]
        xlo = x8[0:ROW_SLAB, :]
        xhi = x8[ROW_SLAB:, :]
        groups = []
        for g in range(npair // SUBLANES):
            p = []
            for k in range(SUBLANES):
                r = pl.multiple_of(idx_ref[t, g * SUBLANES + k], ROW_SLAB)
                lo, hi = _unpack(tab_ref[pl.ds(r, ROW_SLAB), :])
                p.append(lo * xlo + hi * xhi)
            groups.append(_row_sums_8(p, row8))
        col = jnp.sum(jnp.concatenate(groups, axis=0), axis=1, keepdims=True)
        act_scr[...] = jnp.where(lane == t, col, act_scr[...])
        return carry

    lax.fori_loop(0, tu, token, 0)
    coef_ref[...] = gate_ref[...] * jax.nn.gelu(act_scr[...])


def _peer_u_call(idx4, h2v, gate_t, tab, tu):
    b, t, npair = idx4.shape
    nt = t // tu
    return pl.pallas_call(
        _peer_u_kernel,
        grid=(b, nt),
        in_specs=[
            pl.BlockSpec((tu, npair), lambda bi, ti: (bi * nt + ti, 0), memory_space=pltpu.SMEM),
            pl.BlockSpec((None, tu * SUBLANES, LANES), lambda bi, ti: (bi, ti, 0)),
            pl.BlockSpec((None, npair, tu), lambda bi, ti: (bi, 0, ti)),
            pl.BlockSpec(tab.shape, lambda bi, ti: (0, 0), pipeline_mode=pl.Buffered(1)),
        ],
        out_specs=pl.BlockSpec((None, npair, tu), lambda bi, ti: (bi, 0, ti)),
        out_shape=jax.ShapeDtypeStruct((b, npair, t), F32),
        scratch_shapes=[pltpu.VMEM((npair, tu), F32)],
        compiler_params=_params(("parallel", "parallel")),
        name="peer_u",
    )(idx4.reshape(b * t, npair), h2v, gate_t, tab)


N_ACC = 4


def _peer_v_kernel(idx_ref, coef_ref, x_ref, g2_ref, tab_ref, o_ref):
    tv, npair = idx_ref.shape
    g2 = g2_ref[...]

    def token(t, carry):
        acc_lo = [jnp.zeros((ROW_SLAB, LANES), F32) for _ in range(N_ACC)]
        acc_hi = [jnp.zeros((ROW_SLAB, LANES), F32) for _ in range(N_ACC)]
        for k in range(npair):
            r = pl.multiple_of(idx_ref[t, k], ROW_SLAB)
            c = coef_ref[t, k]
            lo, hi = _unpack(tab_ref[pl.ds(r, ROW_SLAB), :])
            acc_lo[k % N_ACC] = acc_lo[k % N_ACC] + c * lo
            acc_hi[k % N_ACC] = acc_hi[k % N_ACC] + c * hi
        out8 = jnp.concatenate([(acc_lo[0] + acc_lo[1]) + (acc_lo[2] + acc_lo[3]),
                                (acc_hi[0] + acc_hi[1]) + (acc_hi[2] + acc_hi[3])], axis=0)
        rr = pl.multiple_of(t * SUBLANES, SUBLANES)
        o_ref[pl.ds(rr, SUBLANES), :] = x_ref[pl.ds(rr, SUBLANES), :] + g2 * out8
        return carry

    lax.fori_loop(0, tv, token, 0)


def _peer_v_call(idx4, coef, x1v, g2v, tab, tv):
    b, t, npair = idx4.shape
    nt = t // tv
    smem = pl.BlockSpec((tv, npair), lambda bi, ti: (bi * nt + ti, 0), memory_space=pltpu.SMEM)
    return pl.pallas_call(
        _peer_v_kernel,
        grid=(b, nt),
        in_specs=[
            smem,
            smem,
            pl.BlockSpec((None, tv * SUBLANES, LANES), lambda bi, ti: (bi, ti, 0)),
            pl.BlockSpec((None, SUBLANES, LANES), lambda bi, ti: (bi, 0, 0)),
            pl.BlockSpec(tab.shape, lambda bi, ti: (0, 0), pipeline_mode=pl.Buffered(1)),
        ],
        out_specs=pl.BlockSpec((None, tv * SUBLANES, LANES), lambda bi, ti: (bi, ti, 0)),
        out_shape=jax.ShapeDtypeStruct(x1v.shape, F32),
        compiler_params=_params(("parallel", "parallel")),
        name="peer_v",
    )(idx4.reshape(b * t, npair), coef.reshape(b * t, npair), x1v, g2v, tab)


def _final_norm_kernel(x_ref, g_ref, o_ref):
    o_ref[...] = _rms(x_ref[...], g_ref[...])


def _final_norm_call(x, g, tb):
    b, t, d = x.shape
    return pl.pallas_call(
        _final_norm_kernel,
        grid=(b, t // tb),
        in_specs=[pl.BlockSpec((None, tb, d), lambda bi, ti: (bi, ti, 0)),
                  pl.BlockSpec((1, d), lambda bi, ti: (0, 0))],
        out_specs=pl.BlockSpec((None, tb, d), lambda bi, ti: (bi, ti, 0)),
        out_shape=jax.ShapeDtypeStruct((b, t, d), F32),
        compiler_params=_params(("parallel", "parallel")),
        name="final_norm",
    )(x, g.reshape(1, d))


def _block_diag(blocks):
    nl, g, r, c = blocks.shape
    eye = jnp.eye(g, dtype=blocks.dtype)
    return jnp.einsum("lgrc,gh->lgrhc", blocks, eye).reshape(nl, g * r, g * c)


def _chunk(t, pref):
    return pref if t % pref == 0 else t


def _run_group(x, mod, conv0, h0, s0r, s0i, w, tabs_u, tabs_v, norm_f):
    b, t, d = x.shape
    nl = mod.shape[0]
    tc = _chunk(t, 128)
    tb = _chunk(t, 128)
    tg = _chunk(t, 64)
    convs, hs, srs, sis = [], [], [], []
    for l in range(nl):
        mod_l = mod[l].reshape(b, 1, mod.shape[-1])
        x1, nc, nh, nsr, nsi = _mixer_call(l, x, mod_l, conv0[l], h0[l], s0r[l], s0i[l], w, tc)
        convs.append(nc)
        hs.append(nh[:, 0])
        srs.append(nsr[:, 0])
        sis.append(nsi[:, 0])
        h2, gate_t, eidx_t = _select_call(l, x1, mod_l, w["norm2"], w["peer_wq"], w["peer_keys"], tb)
        idx4 = jnp.transpose(eidx_t, (0, 2, 1)) * ROW_SLAB
        h2v = h2.reshape(b, t * SUBLANES, LANES)
        coef_t = _peer_u_call(idx4, h2v, gate_t, tabs_u[l], tb)
        coef = jnp.transpose(coef_t, (0, 2, 1))
        g2v = mod_l[:, 0, 5 * d:6 * d].reshape(b, SUBLANES, LANES)
        x2v = _peer_v_call(idx4, coef, x1.reshape(b, t * SUBLANES, LANES), g2v, tabs_v[l], tg)
        x = x2v.reshape(b, t, d)
    y = _final_norm_call(x, norm_f, tb)
    return y, jnp.stack(convs), jnp.stack(hs), jnp.stack(srs), jnp.stack(sis)


def kernel(x_prompt, x_sample, cache_conv, state_lru, state_s5_re, state_s5_im, c_prompt, c_sample,
           w_ada, b_ada, norm1, norm2, w_in, conv_w, conv_b, lru_wa, lru_ba, lru_wx, lru_bx, lru_lam,
           s5_a_re, s5_a_im, s5_log_dt, s5_b_re, s5_b_im, s5_c_re, s5_c_im, s5_d, s5_w_glu, s5_b_glu,
           gn_lru, gn_s5, w_out, peer_wq, peer_keys, peer_u, peer_v, norm_f):
    nl, d, _ = w_ada.shape
    bp = x_prompt.shape[0]
    bs = x_sample.shape[0]
    dl = state_lru.shape[-1]
    g, p = s5_a_re.shape[1:]
    ns = g * p

    mod = _ada_call(jnp.concatenate([c_prompt, c_sample], axis=0), w_ada, b_ada)

    bt_re = jnp.transpose(s5_b_re, (0, 1, 3, 2))
    bt_im = jnp.transpose(s5_b_im, (0, 1, 3, 2))
    ab_re, ab_im, bb_re, bb_im = _s5_disc_call(s5_a_re, s5_a_im, s5_log_dt, bt_re, bt_im)
    b_full = jnp.concatenate([_block_diag(bb_re), _block_diag(bb_im)], axis=-1).astype(BF16)
    c_re = _block_diag(jnp.transpose(s5_c_re, (0, 1, 3, 2))).astype(BF16)
    c_im = _block_diag(jnp.transpose(s5_c_im, (0, 1, 3, 2))).astype(BF16)
    w_gate = jnp.concatenate([_block_diag(lru_wa), _block_diag(lru_wx)], axis=-1).astype(BF16)

    def vec(a):
        return a.reshape(nl, 1, a.shape[-1])

    w = {
        "norm1": vec(norm1), "w_in": w_in.astype(BF16), "conv_w": conv_w, "conv_b": vec(conv_b),
        "w_gate": w_gate, "b_gate": vec(jnp.concatenate([lru_ba, lru_bx], axis=-1)), "lam": vec(lru_lam),
        "ab_re": ab_re.reshape(nl, 1, ns), "ab_im": ab_im.reshape(nl, 1, ns),
        "b_full": b_full, "c_re": c_re, "c_im": c_im, "s5_d": vec(s5_d),
        "w_glu": s5_w_glu.astype(BF16), "b_glu": vec(s5_b_glu), "gn_lru": vec(gn_lru), "gn_s5": vec(gn_s5),
        "w_out": w_out.astype(BF16), "norm2": vec(norm2), "peer_wq": peer_wq.astype(BF16),
        "peer_keys": peer_keys.reshape(nl, 2 * PEER_HEADS, PEER_NKEYS, -1).astype(BF16),
    }
    tab_u = _pack_table(peer_u)
    tab_v = _pack_table(peer_v)

    zeros_conv = jnp.zeros((nl, bp, CONV_WIDTH - 1, dl), F32)
    zeros_lru = jnp.zeros((nl, bp, 1, dl), F32)
    zeros_s5 = jnp.zeros((nl, bp, 1, ns), F32)
    out_p = _run_group(x_prompt, mod[:, :bp], zeros_conv, zeros_lru, zeros_s5, zeros_s5,
                       w, tab_u, tab_v, norm_f)
    out_s = _run_group(x_sample, mod[:, bp:], cache_conv, state_lru.reshape(nl, bs, 1, dl),
                       state_s5_re.reshape(nl, bs, 1, ns), state_s5_im.reshape(nl, bs, 1, ns),
                       w, tab_u, tab_v, norm_f)

    def states(o, bsz):
        y, conv, hl, sr, si = o
        return y, conv, hl, sr.reshape(nl, bsz, g, p), si.reshape(nl, bsz, g, p)

    yp, cp, lp, rp, ip = states(out_p, bp)
    ys, cs, ls, rs, is_ = states(out_s, bs)
    return (yp, ys, cp, lp, rp, ip, cs, ls, rs, is_)
```

```python
import functools

import jax
import jax.numpy as jnp
from jax import lax
from jax.experimental import pallas as pl
from jax.experimental.pallas import tpu as pltpu

F32 = jnp.float32
BF16 = jnp.bfloat16
I32 = jnp.int32

EPS = 1e-6
LRU_C = 8.0
CONV_WIDTH = 4
N_LRU_HEADS = 8
S5_GROUP = 16
PEER_HEADS = 8
PEER_NKEYS = 128
PEER_TOPK = 16
N_PAIRS = PEER_HEADS * PEER_TOPK

LANES = 128
SUBLANES = 8
VMEM_LIMIT = 56 * 1024 * 1024

HI_MASK = -65536


def _rms(x, g):
    return x * lax.rsqrt(jnp.mean(x * x, axis=-1, keepdims=True) + EPS) * g


def _dot(a, b):
    return jnp.dot(a.astype(BF16), b, preferred_element_type=F32)


def _params(sem):
    return pltpu.CompilerParams(dimension_semantics=sem, vmem_limit_bytes=VMEM_LIMIT)


def _ada_kernel(c_ref, w_ref, b_ref, o_ref):
    c = c_ref[...]
    s = c * jax.nn.sigmoid(c)
    o_ref[...] = _dot(s, w_ref[...].astype(BF16)) + b_ref[...]


def _ada_call(c_all, w_ada, b_ada):
    nl, d, d6 = w_ada.shape
    nb = c_all.shape[0]
    bn = d
    return pl.pallas_call(
        _ada_kernel,
        grid=(nl, d6 // bn),
        in_specs=[
            pl.BlockSpec((nb, d), lambda l, j: (0, 0)),
            pl.BlockSpec((None, d, bn), lambda l, j: (l, 0, j)),
            pl.BlockSpec((None, 1, bn), lambda l, j: (l, 0, j)),
        ],
        out_specs=pl.BlockSpec((None, nb, bn), lambda l, j: (l, 0, j)),
        out_shape=jax.ShapeDtypeStruct((nl, nb, d6), F32),
        compiler_params=_params(("parallel", "parallel")),
        name="adaln_mod",
    )(c_all, w_ada, b_ada.reshape(nl, 1, d6))


def _s5_disc_kernel(are_ref, aim_ref, ldt_ref, bre_ref, bim_ref,
                    abre_ref, abim_ref, bbre_ref, bbim_ref):
    lr = are_ref[...]
    li = aim_ref[...]
    dt = jnp.exp(ldt_ref[...])
    mag = jnp.exp(lr * dt)
    ab_re = mag * jnp.cos(li * dt)
    ab_im = mag * jnp.sin(li * dt)
    den = lr * lr + li * li
    nr = ab_re - 1.0
    f_re = (nr * lr + ab_im * li) / den
    f_im = (ab_im * lr - nr * li) / den
    br = bre_ref[...]
    bi = bim_ref[...]
    abre_ref[...] = ab_re
    abim_ref[...] = ab_im
    bbre_ref[...] = f_re * br - f_im * bi
    bbim_ref[...] = f_re * bi + f_im * br


def _s5_disc_call(a_re, a_im, log_dt, bt_re, bt_im):
    nl, g, p = a_re.shape
    gi = bt_re.shape[2]
    spec_a = pl.BlockSpec((None, g, 1, p), lambda l: (l, 0, 0, 0))
    spec_b = pl.BlockSpec((None, g, gi, p), lambda l: (l, 0, 0, 0))
    return pl.pallas_call(
        _s5_disc_kernel,
        grid=(nl,),
        in_specs=[spec_a, spec_a, pl.BlockSpec((None, g, 1, 1), lambda l: (l, 0, 0, 0)), spec_b, spec_b],
        out_specs=[spec_a, spec_a, spec_b, spec_b],
        out_shape=[jax.ShapeDtypeStruct((nl, g, 1, p), F32)] * 2
        + [jax.ShapeDtypeStruct((nl, g, gi, p), F32)] * 2,
        compiler_params=_params(("parallel",)),
        name="s5_discretise",
    )(a_re.reshape(nl, g, 1, p), a_im.reshape(nl, g, 1, p), log_dt.reshape(nl, g, 1, 1), bt_re, bt_im)


def _mixer_kernel(x_ref, mod_ref, n1_ref, win_ref, cw_ref, cb_ref, wg_ref, bg_ref, lam_ref,
                  are_ref, aim_ref, bfull_ref, cre_ref, cim_ref, d_ref, wglu_ref, bglu_ref,
                  gnl_ref, gns_ref, wout_ref, conv0_ref, h0_ref, s0r_ref, s0i_ref,
                  x1_ref, convo_ref, ho_ref, sro_ref, sio_ref,
                  xa_ext, hc, scr, sci, a_scr, b_scr, s_scr):
    tc, d = x_ref.shape
    dl = hc.shape[1]
    tail = SUBLANES
    t = pl.program_id(1)

    @pl.when(t == 0)
    def _():
        xa_ext[0:tail, :] = jnp.zeros((tail, dl), F32)
        xa_ext[tail - (CONV_WIDTH - 1):tail, :] = conv0_ref[...]
        hc[...] = h0_ref[...]
        scr[...] = s0r_ref[...]
        sci[...] = s0i_ref[...]

    x = x_ref[...]
    mod = mod_ref[...]
    sh1, sc1, g1 = mod[:, 0:d], mod[:, d:2 * d], mod[:, 2 * d:3 * d]
    h = _rms(x, n1_ref[...]) * (1.0 + sc1) + sh1
    z = _dot(h, win_ref[...])
    xa = z[:, 0:dl]
    ga = z[:, dl:2 * dl]
    us = z[:, 2 * dl:]

    xa_ext[tail:tail + tc, :] = xa
    cw = cw_ref[...]
    xc = cb_ref[...] + cw[3:4, :] * xa
    for k in range(CONV_WIDTH - 1):
        back = CONV_WIDTH - 1 - k
        xc = xc + cw[k:k + 1, :] * xa_ext[tail - back:tail - back + tc, :]
    xa_ext[0:tail, :] = xa_ext[tc:tc + tail, :]
    convo_ref[...] = xa[tc - (CONV_WIDTH - 1):tc, :]

    rows = lax.broadcasted_iota(I32, (tc, LANES), 0)
    row0 = rows == 0
    ns = scr.shape[1]

    gates = _dot(xc, wg_ref[...]) + bg_ref[...]
    r = jax.nn.sigmoid(gates[:, 0:dl])
    i = jax.nn.sigmoid(gates[:, dl:])
    log_a = -LRU_C * r * jax.nn.softplus(-lam_ref[...])
    a0 = jnp.exp(log_a)
    a_scr[...] = a0
    b_scr[...] = jnp.sqrt(-jnp.tanh(log_a) * (a0 * a0 + 1.0)) * (i * xc)

    def lru_block(j, carry):
        cs = pl.ds(pl.multiple_of(j * LANES, LANES), LANES)
        a = a_scr[:, cs]
        bt = b_scr[:, cs] + jnp.where(row0, a * hc[:, cs], 0.0)
        step = 1
        while step < tc:
            m = rows >= step
            bt = jnp.where(m, a * pltpu.roll(bt, step, 0), 0.0) + bt
            if step * 2 < tc:
                a = jnp.where(m, a * pltpu.roll(a, step, 0), a)
            step *= 2
        b_scr[:, cs] = bt
        return carry

    lax.fori_loop(0, dl // LANES, lru_block, 0)
    hl = b_scr[...]
    hc[...] = hl[tc - 1:tc, :]
    ho_ref[...] = hl[tc - 1:tc, :]
    y_lru = hl * jax.nn.gelu(ga)

    s_scr[...] = _dot(us, bfull_ref[...])

    def s5_block(j, carry):
        cr = pl.ds(pl.multiple_of(j * LANES, LANES), LANES)
        ci = pl.ds(pl.multiple_of(ns + j * LANES, LANES), LANES)
        pr = are_ref[:, cr]
        pi = aim_ref[:, cr]
        c_r = scr[:, cr]
        c_i = sci[:, cr]
        sr = s_scr[:, cr] + jnp.where(row0, pr * c_r - pi * c_i, 0.0)
        si = s_scr[:, ci] + jnp.where(row0, pr * c_i + pi * c_r, 0.0)
        step = 1
        while step < tc:
            m = rows >= step
            qr = pltpu.roll(sr, step, 0)
            qi = pltpu.roll(si, step, 0)
            sr, si = (sr + jnp.where(m, pr * qr - pi * qi, 0.0),
                      si + jnp.where(m, pr * qi + pi * qr, 0.0))
            pr, pi = pr * pr - pi * pi, 2.0 * pr * pi
            step *= 2
        s_scr[:, cr] = sr
        s_scr[:, ci] = si
        return carry

    lax.fori_loop(0, ns // LANES, s5_block, 0)
    sr = s_scr[:, 0:ns]
    si = s_scr[:, ns:]
    scr[...] = sr[tc - 1:tc, :]
    sci[...] = si[tc - 1:tc, :]
    sro_ref[...] = sr[tc - 1:tc, :]
    sio_ref[...] = si[tc - 1:tc, :]
    ys = _dot(sr, cre_ref[...]) - _dot(si, cim_ref[...]) + d_ref[...] * us
    gy = jax.nn.gelu(ys)
    y_s5 = gy * jax.nn.sigmoid(_dot(gy, wglu_ref[...]) + bglu_ref[...])

    wout = wout_ref[...]
    mix = _dot(_rms(y_lru, gnl_ref[...]), wout[0:dl, :]) + _dot(_rms(y_s5, gns_ref[...]), wout[dl:, :])
    x1_ref[...] = x + g1 * mix


def _mixer_call(layer, x, mod, conv0, h0, s0r, s0i, w, tc):
    b, t, d = x.shape
    dl = h0.shape[-1]
    ns = s0r.shape[-1]

    def tok(shape):
        return pl.BlockSpec((None,) + shape, lambda bi, ti: (bi, ti, 0))

    def seq(shape):
        return pl.BlockSpec((None,) + shape, lambda bi, ti: (bi, 0, 0))

    def lay(arr):
        return pl.BlockSpec((None,) + arr.shape[1:], lambda bi, ti: (layer,) + (0,) * (arr.ndim - 1))

    names = ("norm1", "w_in", "conv_w", "conv_b", "w_gate", "b_gate", "lam", "ab_re", "ab_im",
             "b_full", "c_re", "c_im", "s5_d", "w_glu", "b_glu", "gn_lru", "gn_s5", "w_out")
    ws = [w[n] for n in names]
    return pl.pallas_call(
        _mixer_kernel,
        grid=(b, t // tc),
        in_specs=[tok((tc, d)), seq((1, mod.shape[-1]))] + [lay(a) for a in ws]
        + [seq((CONV_WIDTH - 1, dl)), seq((1, dl)), seq((1, ns)), seq((1, ns))],
        out_specs=[tok((tc, d)), seq((CONV_WIDTH - 1, dl)), seq((1, dl)), seq((1, ns)), seq((1, ns))],
        out_shape=[jax.ShapeDtypeStruct((b, t, d), F32),
                   jax.ShapeDtypeStruct((b, CONV_WIDTH - 1, dl), F32),
                   jax.ShapeDtypeStruct((b, 1, dl), F32),
                   jax.ShapeDtypeStruct((b, 1, ns), F32),
                   jax.ShapeDtypeStruct((b, 1, ns), F32)],
        scratch_shapes=[pltpu.VMEM((tc + SUBLANES, dl), F32), pltpu.VMEM((1, dl), F32),
                        pltpu.VMEM((1, ns), F32), pltpu.VMEM((1, ns), F32),
                        pltpu.VMEM((tc, dl), F32), pltpu.VMEM((tc, dl), F32),
                        pltpu.VMEM((tc, 2 * ns), F32)],
        compiler_params=_params(("parallel", "arbitrary")),
        name="mixer",
    )(x, mod, *ws, conv0, h0, s0r, s0i)


def _topk_rows(s, k, payload=None):
    nrow = s.shape[0]
    rows = lax.broadcasted_iota(I32, s.shape, 0)
    vals, outs = [], []
    for _ in range(k):
        m = jnp.max(s, axis=0, keepdims=True)
        idx = jnp.min(jnp.where(s == m, rows, nrow), axis=0, keepdims=True)
        hit = rows == idx
        vals.append(m)
        if payload is None:
            outs.append(idx)
        else:
            outs.append(jnp.max(jnp.where(hit, payload, -1), axis=0, keepdims=True))
        s = jnp.where(hit, -jnp.inf, s)
    return jnp.concatenate(vals, axis=0), jnp.concatenate(outs, axis=0)


def _select_kernel(x_ref, mod_ref, n2_ref, wq_ref, keys_ref, h2_ref, gate_ref, eidx_ref):
    tb, d = x_ref.shape
    x = x_ref[...]
    mod = mod_ref[...]
    sh2, sc2 = mod[:, 3 * d:4 * d], mod[:, 4 * d:5 * d]
    h2 = _rms(x, n2_ref[...]) * (1.0 + sc2) + sh2
    h2_ref[...] = h2
    q = _dot(h2, wq_ref[...])
    dk = keys_ref.shape[2]
    for hd in range(PEER_HEADS):
        vs, ids = [], []
        for c in range(2):
            j = 2 * hd + c
            qhc = q[:, j * dk:(j + 1) * dk].astype(BF16)
            st = lax.dot_general(keys_ref[j], qhc, (((1,), (1,)), ((), ())),
                                 preferred_element_type=F32)
            v, ix = _topk_rows(st, PEER_TOPK)
            vs.append(v)
            ids.append(ix)
        cand = jnp.concatenate([vs[0][a:a + 1, :] + vs[1] for a in range(PEER_TOPK)], axis=0)
        cidx = jnp.concatenate([ids[0][a:a + 1, :] * PEER_NKEYS + ids[1] for a in range(PEER_TOPK)], axis=0)
        top_s, eidx = _topk_rows(cand, PEER_TOPK, payload=cidx)
        e = jnp.exp(top_s - jnp.max(top_s, axis=0, keepdims=True))
        gate = e / jnp.sum(e, axis=0, keepdims=True)
        gate_ref[hd * PEER_TOPK:(hd + 1) * PEER_TOPK, :] = gate
        eidx_ref[hd * PEER_TOPK:(hd + 1) * PEER_TOPK, :] = eidx


def _select_call(layer, x1, mod, norm2, wq, keys, tb):
    b, t, d = x1.shape
    return pl.pallas_call(
        _select_kernel,
        grid=(b, t // tb),
        in_specs=[
            pl.BlockSpec((None, tb, d), lambda bi, ti: (bi, ti, 0)),
            pl.BlockSpec((None, 1, mod.shape[-1]), lambda bi, ti: (bi, 0, 0)),
            pl.BlockSpec((None, 1, d), lambda bi, ti: (layer, 0, 0)),
            pl.BlockSpec((None,) + wq.shape[1:], lambda bi, ti: (layer, 0, 0)),
            pl.BlockSpec((None,) + keys.shape[1:], lambda bi, ti: (layer, 0, 0, 0)),
        ],
        out_specs=[
            pl.BlockSpec((None, tb, d), lambda bi, ti: (bi, ti, 0)),
            pl.BlockSpec((None, N_PAIRS, tb), lambda bi, ti: (bi, 0, ti)),
            pl.BlockSpec((None, N_PAIRS, tb), lambda bi, ti: (bi, 0, ti)),
        ],
        out_shape=[jax.ShapeDtypeStruct((b, t, d), F32),
                   jax.ShapeDtypeStruct((b, N_PAIRS, t), F32),
                   jax.ShapeDtypeStruct((b, N_PAIRS, t), I32)],
        compiler_params=_params(("parallel", "parallel")),
        name="peer_select",
    )(x1, mod, norm2, wq, keys)


ROW_SLAB = 4
GROUP = SUBLANES


def _pack_table(tab):
    nl, ne, d = tab.shape
    bits = lax.bitcast_convert_type(tab.astype(BF16), jnp.uint16).astype(jnp.uint32)
    words = bits[:, :, :d // 2] | (bits[:, :, d // 2:] << 16)
    return lax.bitcast_convert_type(words, I32).reshape(nl, ne * ROW_SLAB, LANES)


def _gather_slabs(idx_ref, t, tab_ref, slab_ref):
    for k in range(idx_ref.shape[1]):
        r = pl.multiple_of(idx_ref[t, k], ROW_SLAB)
        slab_ref[k * ROW_SLAB:(k + 1) * ROW_SLAB, :] = tab_ref[pl.ds(r, ROW_SLAB), :]


def _peer_u_kernel(idx_ref, x_ref, gate_ref, tab_ref, coef_ref, act_scr, slab_a, slab_b):
    tu, npair = idx_ref.shape
    wide = 2 * npair
    row8 = lax.broadcasted_iota(I32, (SUBLANES, LANES), 0)
    rows2 = lax.broadcasted_iota(I32, (SUBLANES, wide), 0)
    lanes2 = lax.broadcasted_iota(I32, (SUBLANES, wide), 1)
    keep = (rows2 < ROW_SLAB) == ((lanes2 & 1) == 0)
    slabs = (slab_a, slab_b)

    def group(g, carry):
        acc = jnp.zeros((SUBLANES, wide), F32)
        for i in range(GROUP):
            t = g * GROUP + i
            slab = slabs[i % 2]
            _gather_slabs(idx_ref, t, tab_ref, slab)
            x8 = x_ref[pl.ds(pl.multiple_of(t * SUBLANES, SUBLANES), SUBLANES), :]
            out = jnp.zeros((SUBLANES, wide), F32)
            for c in range(ROW_SLAB):
                xm = jnp.where((row8 == c) | (row8 == c + ROW_SLAB), x8, 0.0).astype(BF16)
                chunk = pltpu.bitcast(slab[pl.ds(c, npair, stride=ROW_SLAB), :], BF16)
                out = out + lax.dot_general(xm, chunk, (((1,), (1,)), ((), ())),
                                            preferred_element_type=F32)
            z = jnp.sum(jnp.where(keep, out, 0.0), axis=0, keepdims=True)
            acc = jnp.where(rows2 == i, z, acc)
        act = acc + pltpu.roll(acc, wide - 1, 1)
        act_scr[pl.ds(pl.multiple_of(g * GROUP, GROUP), GROUP), :] = act
        return carry

    lax.fori_loop(0, tu // GROUP, group, 0)
    coef_ref[...] = gate_ref[...] * jax.nn.gelu(act_scr[...])


def _peer_u_call(idx4, h2v, gate2, tab, tu):
    b, t, npair = idx4.shape
    nt = t // tu
    wide = gate2.shape[-1]
    return pl.pallas_call(
        _peer_u_kernel,
        grid=(b, nt),
        in_specs=[
            pl.BlockSpec((tu, npair), lambda bi, ti: (bi * nt + ti, 0), memory_space=pltpu.SMEM),
            pl.BlockSpec((None, tu * SUBLANES, LANES), lambda bi, ti: (bi, ti, 0)),
            pl.BlockSpec((None, tu, wide), lambda bi, ti: (bi, ti, 0)),
            pl.BlockSpec(tab.shape, lambda bi, ti: (0, 0), pipeline_mode=pl.Buffered(1)),
        ],
        out_specs=pl.BlockSpec((None, tu, wide), lambda bi, ti: (bi, ti, 0)),
        out_shape=jax.ShapeDtypeStruct((b, t, wide), F32),
        scratch_shapes=[pltpu.VMEM((tu, wide), F32),
                        pltpu.VMEM((npair * ROW_SLAB, LANES), I32),
                        pltpu.VMEM((npair * ROW_SLAB, LANES), I32)],
        compiler_params=_params(("parallel", "parallel")),
        name="peer_u",
    )(idx4.reshape(b * t, npair), h2v, gate2, tab)


def _peer_v_kernel(idx_ref, coef_ref, x_ref, g2_ref, tab_ref, o_ref, cx_scr, slab_a, slab_b):
    tv, npair = idx_ref.shape
    per = 2 * ROW_SLAB
    nk = npair * per
    g2 = g2_ref[...]
    ek = lax.broadcasted_iota(I32, (npair, nk), 0)
    ej = lax.broadcasted_iota(I32, (npair, nk), 1)
    expand = jnp.where((ej >> 3) == ek, 1.0, 0.0).astype(BF16)
    cx_scr[...] = _dot(coef_ref[...], expand)
    mrow = lax.broadcasted_iota(I32, (SUBLANES, nk), 0)
    jj = lax.broadcasted_iota(I32, (SUBLANES, nk), 1) & (per - 1)
    place = mrow == (jj >> 1) + ROW_SLAB * (jj & 1)
    slabs = (slab_a, slab_b)

    def group(g, carry):
        cx8 = cx_scr[pl.ds(pl.multiple_of(g * GROUP, GROUP), GROUP), :]
        for i in range(GROUP):
            t = g * GROUP + i
            slab = slabs[i % 2]
            _gather_slabs(idx_ref, t, tab_ref, slab)
            lhs = jnp.where(place, cx8[i:i + 1, :], 0.0).astype(BF16)
            out8 = jnp.dot(lhs, pltpu.bitcast(slab[...], BF16), preferred_element_type=F32)
            rr = pl.multiple_of(t * SUBLANES, SUBLANES)
            o_ref[pl.ds(rr, SUBLANES), :] = x_ref[pl.ds(rr, SUBLANES), :] + g2 * out8
        return carry

    lax.fori_loop(0, tv // GROUP, group, 0)


def _peer_v_call(idx4, coef, x1v, g2v, tab, tv):
    b, t, npair = idx4.shape
    nt = t // tv
    return pl.pallas_call(
        _peer_v_kernel,
        grid=(b, nt),
        in_specs=[
            pl.BlockSpec((tv, npair), lambda bi, ti: (bi * nt + ti, 0), memory_space=pltpu.SMEM),
            pl.BlockSpec((None, tv, npair), lambda bi, ti: (bi, ti, 0)),
            pl.BlockSpec((None, tv * SUBLANES, LANES), lambda bi, ti: (bi, ti, 0)),
            pl.BlockSpec((None, SUBLANES, LANES), lambda bi, ti: (bi, 0, 0)),
            pl.BlockSpec(tab.shape, lambda bi, ti: (0, 0), pipeline_mode=pl.Buffered(1)),
        ],
        out_specs=pl.BlockSpec((None, tv * SUBLANES, LANES), lambda bi, ti: (bi, ti, 0)),
        out_shape=jax.ShapeDtypeStruct(x1v.shape, F32),
        scratch_shapes=[pltpu.VMEM((tv, npair * 2 * ROW_SLAB), F32),
                        pltpu.VMEM((npair * ROW_SLAB, LANES), I32),
                        pltpu.VMEM((npair * ROW_SLAB, LANES), I32)],
        compiler_params=_params(("parallel", "parallel")),
        name="peer_v",
    )(idx4.reshape(b * t, npair), coef, x1v, g2v, tab)


def _final_norm_kernel(x_ref, g_ref, o_ref):
    o_ref[...] = _rms(x_ref[...], g_ref[...])


def _final_norm_call(x, g, tb):
    b, t, d = x.shape
    return pl.pallas_call(
        _final_norm_kernel,
        grid=(b, t // tb),
        in_specs=[pl.BlockSpec((None, tb, d), lambda bi, ti: (bi, ti, 0)),
                  pl.BlockSpec((1, d), lambda bi, ti: (0, 0))],
        out_specs=pl.BlockSpec((None, tb, d), lambda bi, ti: (bi, ti, 0)),
        out_shape=jax.ShapeDtypeStruct((b, t, d), F32),
        compiler_params=_params(("parallel", "parallel")),
        name="final_norm",
    )(x, g.reshape(1, d))


def _block_diag(blocks):
    nl, g, r, c = blocks.shape
    eye = jnp.eye(g, dtype=blocks.dtype)
    return jnp.einsum("lgrc,gh->lgrhc", blocks, eye).reshape(nl, g * r, g * c)


def _chunk(t, pref):
    return pref if t % pref == 0 else t


def _run_group(x, mod, conv0, h0, s0r, s0i, w, tabs_u, tabs_v, norm_f):
    b, t, d = x.shape
    nl = mod.shape[0]
    tc = _chunk(t, 128)
    tb = _chunk(t, 128)
    convs, hs, srs, sis = [], [], [], []
    for l in range(nl):
        mod_l = mod[l].reshape(b, 1, mod.shape[-1])
        x1, nc, nh, nsr, nsi = _mixer_call(l, x, mod_l, conv0[l], h0[l], s0r[l], s0i[l], w, tc)
        convs.append(nc)
        hs.append(nh[:, 0])
        srs.append(nsr[:, 0])
        sis.append(nsi[:, 0])
        h2, gate_t, eidx_t = _select_call(l, x1, mod_l, w["norm2"], w["peer_wq"], w["peer_keys"], tb)
        idx4 = jnp.transpose(eidx_t, (0, 2, 1)) * ROW_SLAB
        gate = jnp.transpose(gate_t, (0, 2, 1))
        gate2 = jnp.stack([gate, jnp.zeros_like(gate)], axis=-1).reshape(b, t, 2 * N_PAIRS)
        h2v = h2.reshape(b, t * SUBLANES, LANES)
        coef = _peer_u_call(idx4, h2v, gate2, tabs_u[l], tb)[:, :, ::2]
        g2v = mod_l[:, 0, 5 * d:6 * d].reshape(b, SUBLANES, LANES)
        x2v = _peer_v_call(idx4, coef, x1.reshape(b, t * SUBLANES, LANES), g2v, tabs_v[l], tb)
        x = x2v.reshape(b, t, d)
    y = _final_norm_call(x, norm_f, tb)
    return y, jnp.stack(convs), jnp.stack(hs), jnp.stack(srs), jnp.stack(sis)


def kernel(x_prompt, x_sample, cache_conv, state_lru, state_s5_re, state_s5_im, c_prompt, c_sample,
           w_ada, b_ada, norm1, norm2, w_in, conv_w, conv_b, lru_wa, lru_ba, lru_wx, lru_bx, lru_lam,
           s5_a_re, s5_a_im, s5_log_dt, s5_b_re, s5_b_im, s5_c_re, s5_c_im, s5_d, s5_w_glu, s5_b_glu,
           gn_lru, gn_s5, w_out, peer_wq, peer_keys, peer_u, peer_v, norm_f):
    nl, d, _ = w_ada.shape
    bp = x_prompt.shape[0]
    bs = x_sample.shape[0]
    dl = state_lru.shape[-1]
    g, p = s5_a_re.shape[1:]
    ns = g * p

    mod = _ada_call(jnp.concatenate([c_prompt, c_sample], axis=0), w_ada, b_ada)

    bt_re = jnp.transpose(s5_b_re, (0, 1, 3, 2))
    bt_im = jnp.transpose(s5_b_im, (0, 1, 3, 2))
    ab_re, ab_im, bb_re, bb_im = _s5_disc_call(s5_a_re, s5_a_im, s5_log_dt, bt_re, bt_im)
    b_full = jnp.concatenate([_block_diag(bb_re), _block_diag(bb_im)], axis=-1).astype(BF16)
    c_re = _block_diag(jnp.transpose(s5_c_re, (0, 1, 3, 2))).astype(BF16)
    c_im = _block_diag(jnp.transpose(s5_c_im, (0, 1, 3, 2))).astype(BF16)
    w_gate = jnp.concatenate([_block_diag(lru_wa), _block_diag(lru_wx)], axis=-1).astype(BF16)

    def vec(a):
        return a.reshape(nl, 1, a.shape[-1])

    w = {
        "norm1": vec(norm1), "w_in": w_in.astype(BF16), "conv_w": conv_w, "conv_b": vec(conv_b),
        "w_gate": w_gate, "b_gate": vec(jnp.concatenate([lru_ba, lru_bx], axis=-1)), "lam": vec(lru_lam),
        "ab_re": ab_re.reshape(nl, 1, ns), "ab_im": ab_im.reshape(nl, 1, ns),
        "b_full": b_full, "c_re": c_re, "c_im": c_im, "s5_d": vec(s5_d),
        "w_glu": s5_w_glu.astype(BF16), "b_glu": vec(s5_b_glu), "gn_lru": vec(gn_lru), "gn_s5": vec(gn_s5),
        "w_out": w_out.astype(BF16), "norm2": vec(norm2), "peer_wq": peer_wq.astype(BF16),
        "peer_keys": peer_keys.reshape(nl, 2 * PEER_HEADS, PEER_NKEYS, -1).astype(BF16),
    }
    tab_u = _pack_table(peer_u)
    tab_v = _pack_table(peer_v)

    zeros_conv = jnp.zeros((nl, bp, CONV_WIDTH - 1, dl), F32)
    zeros_lru = jnp.zeros((nl, bp, 1, dl), F32)
    zeros_s5 = jnp.zeros((nl, bp, 1, ns), F32)
    out_p = _run_group(x_prompt, mod[:, :bp], zeros_conv, zeros_lru, zeros_s5, zeros_s5,
                       w, tab_u, tab_v, norm_f)
    out_s = _run_group(x_sample, mod[:, bp:], cache_conv, state_lru.reshape(nl, bs, 1, dl),
                       state_s5_re.reshape(nl, bs, 1, ns), state_s5_im.reshape(nl, bs, 1, ns),
                       w, tab_u, tab_v, norm_f)

    def states(o, bsz):
        y, conv, hl, sr, si = o
        return y, conv, hl, sr.reshape(nl, bsz, g, p), si.reshape(nl, bsz, g, p)

    yp, cp, lp, rp, ip = states(out_p, bp)
    ys, cs, ls, rs, is_ = states(out_s, bs)
    return (yp, ys, cp, lp, rp, ip, cs, ls, rs, is_)
```

```python
import functools

import jax
import jax.numpy as jnp
from jax import lax
from jax.experimental import pallas as pl
from jax.experimental.pallas import tpu as pltpu

F32 = jnp.float32
BF16 = jnp.bfloat16
I32 = jnp.int32

EPS = 1e-6
LRU_C = 8.0
CONV_WIDTH = 4
N_LRU_HEADS = 8
S5_GROUP = 16
PEER_HEADS = 8
PEER_NKEYS = 128
PEER_TOPK = 16
N_PAIRS = PEER_HEADS * PEER_TOPK

LANES = 128
SUBLANES = 8
VMEM_LIMIT = 56 * 1024 * 1024

HI_MASK = -65536


def _rms(x, g):
    return x * lax.rsqrt(jnp.mean(x * x, axis=-1, keepdims=True) + EPS) * g


def _dot(a, b):
    return jnp.dot(a.astype(BF16), b, preferred_element_type=F32)


def _params(sem):
    return pltpu.CompilerParams(dimension_semantics=sem, vmem_limit_bytes=VMEM_LIMIT)


def _ada_kernel(c_ref, w_ref, b_ref, o_ref):
    c = c_ref[...]
    s = c * jax.nn.sigmoid(c)
    o_ref[...] = _dot(s, w_ref[...].astype(BF16)) + b_ref[...]


def _ada_call(c_all, w_ada, b_ada):
    nl, d, d6 = w_ada.shape
    nb = c_all.shape[0]
    bn = d
    return pl.pallas_call(
        _ada_kernel,
        grid=(nl, d6 // bn),
        in_specs=[
            pl.BlockSpec((nb, d), lambda l, j: (0, 0)),
            pl.BlockSpec((None, d, bn), lambda l, j: (l, 0, j)),
            pl.BlockSpec((None, 1, bn), lambda l, j: (l, 0, j)),
        ],
        out_specs=pl.BlockSpec((None, nb, bn), lambda l, j: (l, 0, j)),
        out_shape=jax.ShapeDtypeStruct((nl, nb, d6), F32),
        compiler_params=_params(("parallel", "parallel")),
        name="adaln_mod",
    )(c_all, w_ada, b_ada.reshape(nl, 1, d6))


def _s5_disc_kernel(are_ref, aim_ref, ldt_ref, bre_ref, bim_ref,
                    abre_ref, abim_ref, bbre_ref, bbim_ref):
    lr = are_ref[...]
    li = aim_ref[...]
    dt = jnp.exp(ldt_ref[...])
    mag = jnp.exp(lr * dt)
    ab_re = mag * jnp.cos(li * dt)
    ab_im = mag * jnp.sin(li * dt)
    den = lr * lr + li * li
    nr = ab_re - 1.0
    f_re = (nr * lr + ab_im * li) / den
    f_im = (ab_im * lr - nr * li) / den
    br = bre_ref[...]
    bi = bim_ref[...]
    abre_ref[...] = ab_re
    abim_ref[...] = ab_im
    bbre_ref[...] = f_re * br - f_im * bi
    bbim_ref[...] = f_re * bi + f_im * br


def _s5_disc_call(a_re, a_im, log_dt, bt_re, bt_im):
    nl, g, p = a_re.shape
    gi = bt_re.shape[2]
    spec_a = pl.BlockSpec((None, g, 1, p), lambda l: (l, 0, 0, 0))
    spec_b = pl.BlockSpec((None, g, gi, p), lambda l: (l, 0, 0, 0))
    return pl.pallas_call(
        _s5_disc_kernel,
        grid=(nl,),
        in_specs=[spec_a, spec_a, pl.BlockSpec((None, g, 1, 1), lambda l: (l, 0, 0, 0)), spec_b, spec_b],
        out_specs=[spec_a, spec_a, spec_b, spec_b],
        out_shape=[jax.ShapeDtypeStruct((nl, g, 1, p), F32)] * 2
        + [jax.ShapeDtypeStruct((nl, g, gi, p), F32)] * 2,
        compiler_params=_params(("parallel",)),
        name="s5_discretise",
    )(a_re.reshape(nl, g, 1, p), a_im.reshape(nl, g, 1, p), log_dt.reshape(nl, g, 1, 1), bt_re, bt_im)


def _mixer_kernel(x_ref, mod_ref, n1_ref, win_ref, cw_ref, cb_ref, wg_ref, bg_ref, lam_ref,
                  are_ref, aim_ref, bfull_ref, cre_ref, cim_ref, d_ref, wglu_ref, bglu_ref,
                  gnl_ref, gns_ref, wout_ref, conv0_ref, h0_ref, s0r_ref, s0i_ref,
                  x1_ref, convo_ref, ho_ref, sro_ref, sio_ref,
                  xa_ext, hc, scr, sci, a_scr, b_scr, s_scr):
    tc, d = x_ref.shape
    dl = hc.shape[1]
    tail = SUBLANES
    t = pl.program_id(1)

    @pl.when(t == 0)
    def _():
        xa_ext[0:tail, :] = jnp.zeros((tail, dl), F32)
        xa_ext[tail - (CONV_WIDTH - 1):tail, :] = conv0_ref[...]
        hc[...] = h0_ref[...]
        scr[...] = s0r_ref[...]
        sci[...] = s0i_ref[...]

    x = x_ref[...]
    mod = mod_ref[...]
    sh1, sc1, g1 = mod[:, 0:d], mod[:, d:2 * d], mod[:, 2 * d:3 * d]
    h = _rms(x, n1_ref[...]) * (1.0 + sc1) + sh1
    z = _dot(h, win_ref[...])
    xa = z[:, 0:dl]
    ga = z[:, dl:2 * dl]
    us = z[:, 2 * dl:]

    xa_ext[tail:tail + tc, :] = xa
    cw = cw_ref[...]
    xc = cb_ref[...] + cw[3:4, :] * xa
    for k in range(CONV_WIDTH - 1):
        back = CONV_WIDTH - 1 - k
        xc = xc + cw[k:k + 1, :] * xa_ext[tail - back:tail - back + tc, :]
    xa_ext[0:tail, :] = xa_ext[tc:tc + tail, :]
    convo_ref[...] = xa[tc - (CONV_WIDTH - 1):tc, :]

    rows = lax.broadcasted_iota(I32, (tc, LANES), 0)
    row0 = rows == 0
    ns = scr.shape[1]

    gates = _dot(xc, wg_ref[...]) + bg_ref[...]
    r = jax.nn.sigmoid(gates[:, 0:dl])
    i = jax.nn.sigmoid(gates[:, dl:])
    log_a = -LRU_C * r * jax.nn.softplus(-lam_ref[...])
    a0 = jnp.exp(log_a)
    a_scr[...] = a0
    b_scr[...] = jnp.sqrt(-jnp.tanh(log_a) * (a0 * a0 + 1.0)) * (i * xc)

    def lru_block(j, carry):
        cs = pl.ds(pl.multiple_of(j * LANES, LANES), LANES)
        a = a_scr[:, cs]
        bt = b_scr[:, cs] + jnp.where(row0, a * hc[:, cs], 0.0)
        step = 1
        while step < tc:
            m = rows >= step
            bt = jnp.where(m, a * pltpu.roll(bt, step, 0), 0.0) + bt
            if step * 2 < tc:
                a = jnp.where(m, a * pltpu.roll(a, step, 0), a)
            step *= 2
        b_scr[:, cs] = bt
        return carry

    lax.fori_loop(0, dl // LANES, lru_block, 0)
    hl = b_scr[...]
    hc[...] = hl[tc - 1:tc, :]
    ho_ref[...] = hl[tc - 1:tc, :]
    y_lru = hl * jax.nn.gelu(ga)

    s_scr[...] = _dot(us, bfull_ref[...])

    def s5_block(j, carry):
        cr = pl.ds(pl.multiple_of(j * LANES, LANES), LANES)
        ci = pl.ds(pl.multiple_of(ns + j * LANES, LANES), LANES)
        pr = are_ref[:, cr]
        pi = aim_ref[:, cr]
        c_r = scr[:, cr]
        c_i = sci[:, cr]
        sr = s_scr[:, cr] + jnp.where(row0, pr * c_r - pi * c_i, 0.0)
        si = s_scr[:, ci] + jnp.where(row0, pr * c_i + pi * c_r, 0.0)
        step = 1
        while step < tc:
            m = rows >= step
            qr = pltpu.roll(sr, step, 0)
            qi = pltpu.roll(si, step, 0)
            sr, si = (sr + jnp.where(m, pr * qr - pi * qi, 0.0),
                      si + jnp.where(m, pr * qi + pi * qr, 0.0))
            pr, pi = pr * pr - pi * pi, 2.0 * pr * pi
            step *= 2
        s_scr[:, cr] = sr
        s_scr[:, ci] = si
        return carry

    lax.fori_loop(0, ns // LANES, s5_block, 0)
    sr = s_scr[:, 0:ns]
    si = s_scr[:, ns:]
    scr[...] = sr[tc - 1:tc, :]
    sci[...] = si[tc - 1:tc, :]
    sro_ref[...] = sr[tc - 1:tc, :]
    sio_ref[...] = si[tc - 1:tc, :]
    ys = _dot(sr, cre_ref[...]) - _dot(si, cim_ref[...]) + d_ref[...] * us
    gy = jax.nn.gelu(ys)
    y_s5 = gy * jax.nn.sigmoid(_dot(gy, wglu_ref[...]) + bglu_ref[...])

    wout = wout_ref[...]
    mix = _dot(_rms(y_lru, gnl_ref[...]), wout[0:dl, :]) + _dot(_rms(y_s5, gns_ref[...]), wout[dl:, :])
    x1_ref[...] = x + g1 * mix


def _mixer_call(layer, x, mod, conv0, h0, s0r, s0i, w, tc):
    b, t, d = x.shape
    dl = h0.shape[-1]
    ns = s0r.shape[-1]

    def tok(shape):
        return pl.BlockSpec((None,) + shape, lambda bi, ti: (bi, ti, 0))

    def seq(shape):
        return pl.BlockSpec((None,) + shape, lambda bi, ti: (bi, 0, 0))

    def lay(arr):
        return pl.BlockSpec((None,) + arr.shape[1:], lambda bi, ti: (layer,) + (0,) * (arr.ndim - 1))

    names = ("norm1", "w_in", "conv_w", "conv_b", "w_gate", "b_gate", "lam", "ab_re", "ab_im",
             "b_full", "c_re", "c_im", "s5_d", "w_glu", "b_glu", "gn_lru", "gn_s5", "w_out")
    ws = [w[n] for n in names]
    return pl.pallas_call(
        _mixer_kernel,
        grid=(b, t // tc),
        in_specs=[tok((tc, d)), seq((1, mod.shape[-1]))] + [lay(a) for a in ws]
        + [seq((CONV_WIDTH - 1, dl)), seq((1, dl)), seq((1, ns)), seq((1, ns))],
        out_specs=[tok((tc, d)), seq((CONV_WIDTH - 1, dl)), seq((1, dl)), seq((1, ns)), seq((1, ns))],
        out_shape=[jax.ShapeDtypeStruct((b, t, d), F32),
                   jax.ShapeDtypeStruct((b, CONV_WIDTH - 1, dl), F32),
                   jax.ShapeDtypeStruct((b, 1, dl), F32),
                   jax.ShapeDtypeStruct((b, 1, ns), F32),
                   jax.ShapeDtypeStruct((b, 1, ns), F32)],
        scratch_shapes=[pltpu.VMEM((tc + SUBLANES, dl), F32), pltpu.VMEM((1, dl), F32),
                        pltpu.VMEM((1, ns), F32), pltpu.VMEM((1, ns), F32),
                        pltpu.VMEM((tc, dl), F32), pltpu.VMEM((tc, dl), F32),
                        pltpu.VMEM((tc, 2 * ns), F32)],
        compiler_params=_params(("parallel", "arbitrary")),
        name="mixer",
    )(x, mod, *ws, conv0, h0, s0r, s0i)


def _topk_rows(s, k, payload=None):
    nrow = s.shape[0]
    rows = lax.broadcasted_iota(I32, s.shape, 0).astype(F32)
    vals, outs = [], []
    for _ in range(k):
        m = jnp.max(s, axis=0, keepdims=True)
        idx = jnp.min(jnp.where(s == m, rows, float(nrow)), axis=0, keepdims=True)
        hit = rows == idx
        vals.append(m)
        if payload is None:
            outs.append(idx)
        else:
            outs.append(jnp.max(jnp.where(hit, payload, -1.0), axis=0, keepdims=True))
        s = jnp.where(hit, -jnp.inf, s)
    return jnp.concatenate(vals, axis=0), jnp.concatenate(outs, axis=0)


def _select_kernel(x_ref, mod_ref, n2_ref, wq_ref, keys_ref, h2_ref, gate_ref, eidx_ref):
    tb, d = x_ref.shape
    x = x_ref[...]
    mod = mod_ref[...]
    sh2, sc2 = mod[:, 3 * d:4 * d], mod[:, 4 * d:5 * d]
    h2 = _rms(x, n2_ref[...]) * (1.0 + sc2) + sh2
    nchunk = d // LANES
    for c in range(nchunk):
        h2_ref[pl.ds(c, tb, stride=nchunk), :] = h2[:, c * LANES:(c + 1) * LANES]
    q = _dot(h2, wq_ref[...])
    counts = [PEER_TOPK // (a + 1) for a in range(PEER_TOPK)]
    npad = -sum(counts) % SUBLANES
    dk = keys_ref.shape[2]
    for hd in range(PEER_HEADS):
        vs, ids = [], []
        for c in range(2):
            j = 2 * hd + c
            qhc = q[:, j * dk:(j + 1) * dk].astype(BF16)
            st = lax.dot_general(keys_ref[j], qhc, (((1,), (1,)), ((), ())),
                                 preferred_element_type=F32)
            v, ix = _topk_rows(st, PEER_TOPK)
            vs.append(v)
            ids.append(ix)
        cand = jnp.concatenate(
            [vs[0][a:a + 1, :] + vs[1][0:counts[a], :] for a in range(PEER_TOPK)]
            + [jnp.full((npad, tb), -jnp.inf, F32)], axis=0)
        cidx = jnp.concatenate(
            [ids[0][a:a + 1, :] * float(PEER_NKEYS) + ids[1][0:counts[a], :] for a in range(PEER_TOPK)]
            + [jnp.zeros((npad, tb), F32)], axis=0)
        top_s, eidx = _topk_rows(cand, PEER_TOPK, payload=cidx)
        e = jnp.exp(top_s - jnp.max(top_s, axis=0, keepdims=True))
        gate = e / jnp.sum(e, axis=0, keepdims=True)
        gate_ref[hd * PEER_TOPK:(hd + 1) * PEER_TOPK, :] = gate
        eidx_ref[hd * PEER_TOPK:(hd + 1) * PEER_TOPK, :] = eidx.astype(I32)


def _select_call(layer, x1, mod, norm2, wq, keys, tb):
    b, t, d = x1.shape
    return pl.pallas_call(
        _select_kernel,
        grid=(b, t // tb),
        in_specs=[
            pl.BlockSpec((None, tb, d), lambda bi, ti: (bi, ti, 0)),
            pl.BlockSpec((None, 1, mod.shape[-1]), lambda bi, ti: (bi, 0, 0)),
            pl.BlockSpec((None, 1, d), lambda bi, ti: (layer, 0, 0)),
            pl.BlockSpec((None,) + wq.shape[1:], lambda bi, ti: (layer, 0, 0)),
            pl.BlockSpec((None,) + keys.shape[1:], lambda bi, ti: (layer, 0, 0, 0)),
        ],
        out_specs=[
            pl.BlockSpec((None, tb * (d // LANES), LANES), lambda bi, ti: (bi, ti, 0)),
            pl.BlockSpec((None, N_PAIRS, tb), lambda bi, ti: (bi, 0, ti)),
            pl.BlockSpec((None, N_PAIRS, tb), lambda bi, ti: (bi, 0, ti)),
        ],
        out_shape=[jax.ShapeDtypeStruct((b, t * (d // LANES), LANES), F32),
                   jax.ShapeDtypeStruct((b, N_PAIRS, t), F32),
                   jax.ShapeDtypeStruct((b, N_PAIRS, t), I32)],
        compiler_params=_params(("parallel", "parallel")),
        name="peer_select",
    )(x1, mod, norm2, wq, keys)


ROW_SLAB = 4
GROUP = SUBLANES


def _pack_table(tab):
    nl, ne, d = tab.shape
    bits = lax.bitcast_convert_type(tab.astype(BF16), jnp.uint16).astype(jnp.uint32)
    words = bits[:, :, :d // 2] | (bits[:, :, d // 2:] << 16)
    return lax.bitcast_convert_type(words, I32).reshape(nl, ne * ROW_SLAB, LANES)


def _gather_slabs(idx_ref, t, tab_ref, slab_ref):
    for k in range(idx_ref.shape[1]):
        r = pl.multiple_of(idx_ref[t, k], ROW_SLAB)
        slab_ref[k * ROW_SLAB:(k + 1) * ROW_SLAB, :] = tab_ref[pl.ds(r, ROW_SLAB), :]


def _peer_u_kernel(idx_ref, x_ref, gate_ref, tab_ref, coef_ref, act_scr, slab_a, slab_b):
    tu, npair = idx_ref.shape
    wide = 2 * npair
    row8 = lax.broadcasted_iota(I32, (SUBLANES, LANES), 0)
    rows2 = lax.broadcasted_iota(I32, (SUBLANES, wide), 0)
    lanes2 = lax.broadcasted_iota(I32, (SUBLANES, wide), 1)
    keep = (rows2 < ROW_SLAB) == ((lanes2 & 1) == 0)
    slabs = (slab_a, slab_b)

    def group(g, carry):
        acc = jnp.zeros((SUBLANES, wide), F32)
        for i in range(GROUP):
            t = g * GROUP + i
            slab = slabs[i % 2]
            _gather_slabs(idx_ref, t, tab_ref, slab)
            x8 = x_ref[pl.ds(pl.multiple_of(t * SUBLANES, SUBLANES), SUBLANES), :]
            out = jnp.zeros((SUBLANES, wide), F32)
            for c in range(ROW_SLAB):
                xm = jnp.where((row8 == c) | (row8 == c + ROW_SLAB), x8, 0.0).astype(BF16)
                chunk = pltpu.bitcast(slab[pl.ds(c, npair, stride=ROW_SLAB), :], BF16)
                out = out + lax.dot_general(xm, chunk, (((1,), (1,)), ((), ())),
                                            preferred_element_type=F32)
            z = jnp.sum(jnp.where(keep, out, 0.0), axis=0, keepdims=True)
            acc = jnp.where(rows2 == i, z, acc)
        act = acc + pltpu.roll(acc, wide - 1, 1)
        act_scr[pl.ds(pl.multiple_of(g * GROUP, GROUP), GROUP), :] = act
        return carry

    lax.fori_loop(0, tu // GROUP, group, 0)
    coef_ref[...] = gate_ref[...] * jax.nn.gelu(act_scr[...])


def _peer_u_call(idx4, h2v, gate2, tab, tu):
    b, t, npair = idx4.shape
    nt = t // tu
    wide = gate2.shape[-1]
    return pl.pallas_call(
        _peer_u_kernel,
        grid=(b, nt),
        in_specs=[
            pl.BlockSpec((tu, npair), lambda bi, ti: (bi * nt + ti, 0), memory_space=pltpu.SMEM),
            pl.BlockSpec((None, tu * SUBLANES, LANES), lambda bi, ti: (bi, ti, 0)),
            pl.BlockSpec((None, tu, wide), lambda bi, ti: (bi, ti, 0)),
            pl.BlockSpec(tab.shape, lambda bi, ti: (0, 0), pipeline_mode=pl.Buffered(1)),
        ],
        out_specs=pl.BlockSpec((None, tu, wide), lambda bi, ti: (bi, ti, 0)),
        out_shape=jax.ShapeDtypeStruct((b, t, wide), F32),
        scratch_shapes=[pltpu.VMEM((tu, wide), F32),
                        pltpu.VMEM((npair * ROW_SLAB, LANES), I32),
                        pltpu.VMEM((npair * ROW_SLAB, LANES), I32)],
        compiler_params=_params(("parallel", "parallel")),
        name="peer_u",
    )(idx4.reshape(b * t, npair), h2v, gate2, tab)


def _peer_v_kernel(idx_ref, coef_ref, x_ref, mod_ref, tab_ref, o_ref, cx_scr, out_scr, slab_a, slab_b):
    tv, npair = idx_ref.shape
    d = x_ref.shape[1]
    per = 2 * ROW_SLAB
    nk = npair * per
    ek = lax.broadcasted_iota(I32, (2 * npair, nk), 0)
    ej = lax.broadcasted_iota(I32, (2 * npair, nk), 1)
    expand = jnp.where((ej >> 3) * 2 == ek, 1.0, 0.0).astype(BF16)
    cx_scr[...] = _dot(coef_ref[...], expand)
    mrow = lax.broadcasted_iota(I32, (SUBLANES, nk), 0)
    jj = lax.broadcasted_iota(I32, (SUBLANES, nk), 1) & (per - 1)
    place = mrow == (jj >> 1) + ROW_SLAB * (jj & 1)
    slabs = (slab_a, slab_b)

    def group(g, carry):
        cx8 = cx_scr[pl.ds(pl.multiple_of(g * GROUP, GROUP), GROUP), :]
        for i in range(GROUP):
            t = g * GROUP + i
            slab = slabs[i % 2]
            _gather_slabs(idx_ref, t, tab_ref, slab)
            lhs = jnp.where(place, cx8[i:i + 1, :], 0.0).astype(BF16)
            out8 = jnp.dot(lhs, pltpu.bitcast(slab[...], BF16), preferred_element_type=F32)
            out_scr[pl.ds(pl.multiple_of(t * SUBLANES, SUBLANES), SUBLANES), :] = out8
        return carry

    lax.fori_loop(0, tv // GROUP, group, 0)
    g2 = mod_ref[:, 5 * d:6 * d]
    nchunk = d // LANES
    for c in range(nchunk):
        cs = slice(c * LANES, (c + 1) * LANES)
        o_ref[:, cs] = x_ref[:, cs] + g2[:, cs] * out_scr[pl.ds(c, tv, stride=nchunk), :]


def _peer_v_call(idx4, coef2, x1, mod, tab, tv):
    b, t, npair = idx4.shape
    d = x1.shape[-1]
    nt = t // tv
    return pl.pallas_call(
        _peer_v_kernel,
        grid=(b, nt),
        in_specs=[
            pl.BlockSpec((tv, npair), lambda bi, ti: (bi * nt + ti, 0), memory_space=pltpu.SMEM),
            pl.BlockSpec((None, tv, 2 * npair), lambda bi, ti: (bi, ti, 0)),
            pl.BlockSpec((None, tv, d), lambda bi, ti: (bi, ti, 0)),
            pl.BlockSpec((None, 1, mod.shape[-1]), lambda bi, ti: (bi, 0, 0)),
            pl.BlockSpec(tab.shape, lambda bi, ti: (0, 0), pipeline_mode=pl.Buffered(1)),
        ],
        out_specs=pl.BlockSpec((None, tv, d), lambda bi, ti: (bi, ti, 0)),
        out_shape=jax.ShapeDtypeStruct(x1.shape, F32),
        scratch_shapes=[pltpu.VMEM((tv, npair * 2 * ROW_SLAB), F32),
                        pltpu.VMEM((tv * SUBLANES, LANES), F32),
                        pltpu.VMEM((npair * ROW_SLAB, LANES), I32),
                        pltpu.VMEM((npair * ROW_SLAB, LANES), I32)],
        compiler_params=_params(("parallel", "parallel")),
        name="peer_v",
    )(idx4.reshape(b * t, npair), coef2, x1, mod, tab)


def _final_norm_kernel(x_ref, g_ref, o_ref):
    o_ref[...] = _rms(x_ref[...], g_ref[...])


def _final_norm_call(x, g, tb):
    b, t, d = x.shape
    return pl.pallas_call(
        _final_norm_kernel,
        grid=(b, t // tb),
        in_specs=[pl.BlockSpec((None, tb, d), lambda bi, ti: (bi, ti, 0)),
                  pl.BlockSpec((1, d), lambda bi, ti: (0, 0))],
        out_specs=pl.BlockSpec((None, tb, d), lambda bi, ti: (bi, ti, 0)),
        out_shape=jax.ShapeDtypeStruct((b, t, d), F32),
        compiler_params=_params(("parallel", "parallel")),
        name="final_norm",
    )(x, g.reshape(1, d))


def _block_diag(blocks):
    nl, g, r, c = blocks.shape
    eye = jnp.eye(g, dtype=blocks.dtype)
    return jnp.einsum("lgrc,gh->lgrhc", blocks, eye).reshape(nl, g * r, g * c)


def _chunk(t, pref):
    return pref if t % pref == 0 else t


def _run_group(x, mod, conv0, h0, s0r, s0i, w, tabs_u, tabs_v, norm_f):
    b, t, d = x.shape
    nl = mod.shape[0]
    tc = _chunk(t, 128)
    tb = _chunk(t, 128)
    convs, hs, srs, sis = [], [], [], []
    for l in range(nl):
        mod_l = mod[l].reshape(b, 1, mod.shape[-1])
        x1, nc, nh, nsr, nsi = _mixer_call(l, x, mod_l, conv0[l], h0[l], s0r[l], s0i[l], w, tc)
        convs.append(nc)
        hs.append(nh[:, 0])
        srs.append(nsr[:, 0])
        sis.append(nsi[:, 0])
        h2v, gate_t, eidx_t = _select_call(l, x1, mod_l, w["norm2"], w["peer_wq"], w["peer_keys"], tb)
        idx4 = jnp.transpose(eidx_t, (0, 2, 1)) * ROW_SLAB
        gate = jnp.transpose(gate_t, (0, 2, 1))
        gate2 = jnp.stack([gate, jnp.zeros_like(gate)], axis=-1).reshape(b, t, 2 * N_PAIRS)
        coef2 = _peer_u_call(idx4, h2v, gate2, tabs_u[l], tb)
        x = _peer_v_call(idx4, coef2, x1, mod_l, tabs_v[l], tb)
    y = _final_norm_call(x, norm_f, tb)
    return y, jnp.stack(convs), jnp.stack(hs), jnp.stack(srs), jnp.stack(sis)


def kernel(x_prompt, x_sample, cache_conv, state_lru, state_s5_re, state_s5_im, c_prompt, c_sample,
           w_ada, b_ada, norm1, norm2, w_in, conv_w, conv_b, lru_wa, lru_ba, lru_wx, lru_bx, lru_lam,
           s5_a_re, s5_a_im, s5_log_dt, s5_b_re, s5_b_im, s5_c_re, s5_c_im, s5_d, s5_w_glu, s5_b_glu,
           gn_lru, gn_s5, w_out, peer_wq, peer_keys, peer_u, peer_v, norm_f):
    nl, d, _ = w_ada.shape
    bp = x_prompt.shape[0]
    bs = x_sample.shape[0]
    dl = state_lru.shape[-1]
    g, p = s5_a_re.shape[1:]
    ns = g * p

    mod = _ada_call(jnp.concatenate([c_prompt, c_sample], axis=0), w_ada, b_ada)

    bt_re = jnp.transpose(s5_b_re, (0, 1, 3, 2))
    bt_im = jnp.transpose(s5_b_im, (0, 1, 3, 2))
    ab_re, ab_im, bb_re, bb_im = _s5_disc_call(s5_a_re, s5_a_im, s5_log_dt, bt_re, bt_im)
    b_full = jnp.concatenate([_block_diag(bb_re), _block_diag(bb_im)], axis=-1).astype(BF16)
    c_re = _block_diag(jnp.transpose(s5_c_re, (0, 1, 3, 2))).astype(BF16)
    c_im = _block_diag(jnp.transpose(s5_c_im, (0, 1, 3, 2))).astype(BF16)
    w_gate = jnp.concatenate([_block_diag(lru_wa), _block_diag(lru_wx)], axis=-1).astype(BF16)

    def vec(a):
        return a.reshape(nl, 1, a.shape[-1])

    w = {
        "norm1": vec(norm1), "w_in": w_in.astype(BF16), "conv_w": conv_w, "conv_b": vec(conv_b),
        "w_gate": w_gate, "b_gate": vec(jnp.concatenate([lru_ba, lru_bx], axis=-1)), "lam": vec(lru_lam),
        "ab_re": ab_re.reshape(nl, 1, ns), "ab_im": ab_im.reshape(nl, 1, ns),
        "b_full": b_full, "c_re": c_re, "c_im": c_im, "s5_d": vec(s5_d),
        "w_glu": s5_w_glu.astype(BF16), "b_glu": vec(s5_b_glu), "gn_lru": vec(gn_lru), "gn_s5": vec(gn_s5),
        "w_out": w_out.astype(BF16), "norm2": vec(norm2), "peer_wq": peer_wq.astype(BF16),
        "peer_keys": peer_keys.reshape(nl, 2 * PEER_HEADS, PEER_NKEYS, -1).astype(BF16),
    }
    tab_u = _pack_table(peer_u)
    tab_v = _pack_table(peer_v)

    zeros_conv = jnp.zeros((nl, bp, CONV_WIDTH - 1, dl), F32)
    zeros_lru = jnp.zeros((nl, bp, 1, dl), F32)
    zeros_s5 = jnp.zeros((nl, bp, 1, ns), F32)
    out_p = _run_group(x_prompt, mod[:, :bp], zeros_conv, zeros_lru, zeros_s5, zeros_s5,
                       w, tab_u, tab_v, norm_f)
    out_s = _run_group(x_sample, mod[:, bp:], cache_conv, state_lru.reshape(nl, bs, 1, dl),
                       state_s5_re.reshape(nl, bs, 1, ns), state_s5_im.reshape(nl, bs, 1, ns),
                       w, tab_u, tab_v, norm_f)

    def states(o, bsz):
        y, conv, hl, sr, si = o
        return y, conv, hl, sr.reshape(nl, bsz, g, p), si.reshape(nl, bsz, g, p)

    yp, cp, lp, rp, ip = states(out_p, bp)
    ys, cs, ls, rs, is_ = states(out_s, bs)
    return (yp, ys, cp, lp, rp, ip, cs, ls, rs, is_)
```

```python
import functools

import jax
import jax.numpy as jnp
from jax import lax
from jax.experimental import pallas as pl
from jax.experimental.pallas import tpu as pltpu

F32 = jnp.float32
BF16 = jnp.bfloat16
I32 = jnp.int32

EPS = 1e-6
LRU_C = 8.0
CONV_WIDTH = 4
N_LRU_HEADS = 8
S5_GROUP = 16
PEER_HEADS = 8
PEER_NKEYS = 128
PEER_TOPK = 16
N_PAIRS = PEER_HEADS * PEER_TOPK

LANES = 128
SUBLANES = 8
VMEM_LIMIT = 56 * 1024 * 1024

HI_MASK = -65536


def _rms(x, g):
    return x * lax.rsqrt(jnp.mean(x * x, axis=-1, keepdims=True) + EPS) * g


def _dot(a, b):
    return jnp.dot(a.astype(BF16), b, preferred_element_type=F32)


def _params(sem):
    return pltpu.CompilerParams(dimension_semantics=sem, vmem_limit_bytes=VMEM_LIMIT)


def _ada_kernel(c_ref, w_ref, b_ref, o_ref):
    c = c_ref[...]
    s = c * jax.nn.sigmoid(c)
    o_ref[...] = _dot(s, w_ref[...].astype(BF16)) + b_ref[...]


def _ada_call(c_all, w_ada, b_ada):
    nl, d, d6 = w_ada.shape
    nb = c_all.shape[0]
    bn = d
    return pl.pallas_call(
        _ada_kernel,
        grid=(nl, d6 // bn),
        in_specs=[
            pl.BlockSpec((nb, d), lambda l, j: (0, 0)),
            pl.BlockSpec((None, d, bn), lambda l, j: (l, 0, j)),
            pl.BlockSpec((None, 1, bn), lambda l, j: (l, 0, j)),
        ],
        out_specs=pl.BlockSpec((None, nb, bn), lambda l, j: (l, 0, j)),
        out_shape=jax.ShapeDtypeStruct((nl, nb, d6), F32),
        compiler_params=_params(("parallel", "parallel")),
        name="adaln_mod",
    )(c_all, w_ada, b_ada.reshape(nl, 1, d6))


def _s5_disc_kernel(are_ref, aim_ref, ldt_ref, bre_ref, bim_ref,
                    abre_ref, abim_ref, bbre_ref, bbim_ref):
    lr = are_ref[...]
    li = aim_ref[...]
    dt = jnp.exp(ldt_ref[...])
    mag = jnp.exp(lr * dt)
    ab_re = mag * jnp.cos(li * dt)
    ab_im = mag * jnp.sin(li * dt)
    den = lr * lr + li * li
    nr = ab_re - 1.0
    f_re = (nr * lr + ab_im * li) / den
    f_im = (ab_im * lr - nr * li) / den
    br = bre_ref[...]
    bi = bim_ref[...]
    abre_ref[...] = ab_re
    abim_ref[...] = ab_im
    bbre_ref[...] = f_re * br - f_im * bi
    bbim_ref[...] = f_re * bi + f_im * br


def _s5_disc_call(a_re, a_im, log_dt, bt_re, bt_im):
    nl, g, p = a_re.shape
    gi = bt_re.shape[2]
    spec_a = pl.BlockSpec((None, g, 1, p), lambda l: (l, 0, 0, 0))
    spec_b = pl.BlockSpec((None, g, gi, p), lambda l: (l, 0, 0, 0))
    return pl.pallas_call(
        _s5_disc_kernel,
        grid=(nl,),
        in_specs=[spec_a, spec_a, pl.BlockSpec((None, g, 1, 1), lambda l: (l, 0, 0, 0)), spec_b, spec_b],
        out_specs=[spec_a, spec_a, spec_b, spec_b],
        out_shape=[jax.ShapeDtypeStruct((nl, g, 1, p), F32)] * 2
        + [jax.ShapeDtypeStruct((nl, g, gi, p), F32)] * 2,
        compiler_params=_params(("parallel",)),
        name="s5_discretise",
    )(a_re.reshape(nl, g, 1, p), a_im.reshape(nl, g, 1, p), log_dt.reshape(nl, g, 1, 1), bt_re, bt_im)


def _mixer_kernel(x_ref, mod_ref, n1_ref, win_ref, cw_ref, cb_ref, wg_ref, bg_ref, lam_ref,
                  are_ref, aim_ref, bfull_ref, cre_ref, cim_ref, d_ref, wglu_ref, bglu_ref,
                  gnl_ref, gns_ref, wout_ref, conv0_ref, h0_ref, s0r_ref, s0i_ref,
                  x1_ref, convo_ref, ho_ref, sro_ref, sio_ref,
                  xa_ext, hc, scr, sci, a_scr, b_scr, s_scr):
    tc, d = x_ref.shape
    dl = hc.shape[1]
    tail = SUBLANES
    t = pl.program_id(1)

    @pl.when(t == 0)
    def _():
        xa_ext[0:tail, :] = jnp.zeros((tail, dl), F32)
        xa_ext[tail - (CONV_WIDTH - 1):tail, :] = conv0_ref[...]
        hc[...] = h0_ref[...]
        scr[...] = s0r_ref[...]
        sci[...] = s0i_ref[...]

    x = x_ref[...]
    mod = mod_ref[...]
    sh1, sc1, g1 = mod[:, 0:d], mod[:, d:2 * d], mod[:, 2 * d:3 * d]
    h = _rms(x, n1_ref[...]) * (1.0 + sc1) + sh1
    z = _dot(h, win_ref[...])
    xa = z[:, 0:dl]
    ga = z[:, dl:2 * dl]
    us = z[:, 2 * dl:]

    xa_ext[tail:tail + tc, :] = xa
    cw = cw_ref[...]
    xc = cb_ref[...] + cw[3:4, :] * xa
    for k in range(CONV_WIDTH - 1):
        back = CONV_WIDTH - 1 - k
        xc = xc + cw[k:k + 1, :] * xa_ext[tail - back:tail - back + tc, :]
    xa_ext[0:tail, :] = xa_ext[tc:tc + tail, :]
    convo_ref[...] = xa[tc - (CONV_WIDTH - 1):tc, :]

    rows = lax.broadcasted_iota(I32, (tc, LANES), 0)
    row0 = rows == 0
    ns = scr.shape[1]

    gates = _dot(xc, wg_ref[...]) + bg_ref[...]
    r = jax.nn.sigmoid(gates[:, 0:dl])
    i = jax.nn.sigmoid(gates[:, dl:])
    log_a = -LRU_C * r * jax.nn.softplus(-lam_ref[...])
    a0 = jnp.exp(log_a)
    a_scr[...] = a0
    b_scr[...] = jnp.sqrt(-jnp.tanh(log_a) * (a0 * a0 + 1.0)) * (i * xc)

    def lru_block(j, carry):
        cs = pl.ds(pl.multiple_of(j * LANES, LANES), LANES)
        a = a_scr[:, cs]
        bt = b_scr[:, cs] + jnp.where(row0, a * hc[:, cs], 0.0)
        step = 1
        while step < tc:
            m = rows >= step
            bt = jnp.where(m, a * pltpu.roll(bt, step, 0), 0.0) + bt
            if step * 2 < tc:
                a = jnp.where(m, a * pltpu.roll(a, step, 0), a)
            step *= 2
        b_scr[:, cs] = bt
        return carry

    lax.fori_loop(0, dl // LANES, lru_block, 0)
    hl = b_scr[...]
    hc[...] = hl[tc - 1:tc, :]
    ho_ref[...] = hl[tc - 1:tc, :]
    y_lru = hl * jax.nn.gelu(ga)

    s_scr[...] = _dot(us, bfull_ref[...])

    def s5_block(j, carry):
        cr = pl.ds(pl.multiple_of(j * LANES, LANES), LANES)
        ci = pl.ds(pl.multiple_of(ns + j * LANES, LANES), LANES)
        pr = are_ref[:, cr]
        pi = aim_ref[:, cr]
        c_r = scr[:, cr]
        c_i = sci[:, cr]
        sr = s_scr[:, cr] + jnp.where(row0, pr * c_r - pi * c_i, 0.0)
        si = s_scr[:, ci] + jnp.where(row0, pr * c_i + pi * c_r, 0.0)
        step = 1
        while step < tc:
            m = rows >= step
            qr = pltpu.roll(sr, step, 0)
            qi = pltpu.roll(si, step, 0)
            sr, si = (sr + jnp.where(m, pr * qr - pi * qi, 0.0),
                      si + jnp.where(m, pr * qi + pi * qr, 0.0))
            pr, pi = pr * pr - pi * pi, 2.0 * pr * pi
            step *= 2
        s_scr[:, cr] = sr
        s_scr[:, ci] = si
        return carry

    lax.fori_loop(0, ns // LANES, s5_block, 0)
    sr = s_scr[:, 0:ns]
    si = s_scr[:, ns:]
    scr[...] = sr[tc - 1:tc, :]
    sci[...] = si[tc - 1:tc, :]
    sro_ref[...] = sr[tc - 1:tc, :]
    sio_ref[...] = si[tc - 1:tc, :]
    ys = _dot(sr, cre_ref[...]) - _dot(si, cim_ref[...]) + d_ref[...] * us
    gy = jax.nn.gelu(ys)
    y_s5 = gy * jax.nn.sigmoid(_dot(gy, wglu_ref[...]) + bglu_ref[...])

    wout = wout_ref[...]
    mix = _dot(_rms(y_lru, gnl_ref[...]), wout[0:dl, :]) + _dot(_rms(y_s5, gns_ref[...]), wout[dl:, :])
    x1_ref[...] = x + g1 * mix


def _mixer_call(layer, x, mod, conv0, h0, s0r, s0i, w, tc):
    b, t, d = x.shape
    dl = h0.shape[-1]
    ns = s0r.shape[-1]

    def tok(shape):
        return pl.BlockSpec((None,) + shape, lambda bi, ti: (bi, ti, 0))

    def seq(shape):
        return pl.BlockSpec((None,) + shape, lambda bi, ti: (bi, 0, 0))

    def lay(arr):
        return pl.BlockSpec((None,) + arr.shape[1:], lambda bi, ti: (layer,) + (0,) * (arr.ndim - 1))

    names = ("norm1", "w_in", "conv_w", "conv_b", "w_gate", "b_gate", "lam", "ab_re", "ab_im",
             "b_full", "c_re", "c_im", "s5_d", "w_glu", "b_glu", "gn_lru", "gn_s5", "w_out")
    ws = [w[n] for n in names]
    return pl.pallas_call(
        _mixer_kernel,
        grid=(b, t // tc),
        in_specs=[tok((tc, d)), seq((1, mod.shape[-1]))] + [lay(a) for a in ws]
        + [seq((CONV_WIDTH - 1, dl)), seq((1, dl)), seq((1, ns)), seq((1, ns))],
        out_specs=[tok((tc, d)), seq((CONV_WIDTH - 1, dl)), seq((1, dl)), seq((1, ns)), seq((1, ns))],
        out_shape=[jax.ShapeDtypeStruct((b, t, d), F32),
                   jax.ShapeDtypeStruct((b, CONV_WIDTH - 1, dl), F32),
                   jax.ShapeDtypeStruct((b, 1, dl), F32),
                   jax.ShapeDtypeStruct((b, 1, ns), F32),
                   jax.ShapeDtypeStruct((b, 1, ns), F32)],
        scratch_shapes=[pltpu.VMEM((tc + SUBLANES, dl), F32), pltpu.VMEM((1, dl), F32),
                        pltpu.VMEM((1, ns), F32), pltpu.VMEM((1, ns), F32),
                        pltpu.VMEM((tc, dl), F32), pltpu.VMEM((tc, dl), F32),
                        pltpu.VMEM((tc, 2 * ns), F32)],
        compiler_params=_params(("parallel", "arbitrary")),
        name="mixer",
    )(x, mod, *ws, conv0, h0, s0r, s0i)


def _topk_rows(s, k, payload=None):
    nrow = s.shape[0]
    rows = lax.broadcasted_iota(I32, s.shape, 0).astype(F32)
    vals, outs = [], []
    for _ in range(k):
        m = jnp.max(s, axis=0, keepdims=True)
        idx = jnp.min(jnp.where(s == m, rows, float(nrow)), axis=0, keepdims=True)
        hit = rows == idx
        vals.append(m)
        if payload is None:
            outs.append(idx)
        else:
            outs.append(jnp.max(jnp.where(hit, payload, -1.0), axis=0, keepdims=True))
        s = jnp.where(hit, -jnp.inf, s)
    return jnp.concatenate(vals, axis=0), jnp.concatenate(outs, axis=0)


def _select_kernel(x_ref, mod_ref, n2_ref, wq_ref, keys_ref, h2_ref, gate_ref, eidx_ref):
    tb, d = x_ref.shape
    x = x_ref[...]
    mod = mod_ref[...]
    sh2, sc2 = mod[:, 3 * d:4 * d], mod[:, 4 * d:5 * d]
    h2 = _rms(x, n2_ref[...]) * (1.0 + sc2) + sh2
    nchunk = d // LANES
    for c in range(nchunk):
        h2_ref[pl.ds(c, tb, stride=nchunk), :] = h2[:, c * LANES:(c + 1) * LANES]
    q = _dot(h2, wq_ref[...])
    counts = [PEER_TOPK // (a + 1) for a in range(PEER_TOPK)]
    npad = -sum(counts) % SUBLANES
    dk = keys_ref.shape[2]
    for hd in range(PEER_HEADS):
        vs, ids = [], []
        for c in range(2):
            j = 2 * hd + c
            qhc = q[:, j * dk:(j + 1) * dk].astype(BF16)
            st = lax.dot_general(keys_ref[j], qhc, (((1,), (1,)), ((), ())),
                                 preferred_element_type=F32)
            v, ix = _topk_rows(st, PEER_TOPK)
            vs.append(v)
            ids.append(ix)
        cand = jnp.concatenate(
            [vs[0][a:a + 1, :] + vs[1][0:counts[a], :] for a in range(PEER_TOPK)]
            + [jnp.full((npad, tb), -jnp.inf, F32)], axis=0)
        cidx = jnp.concatenate(
            [ids[0][a:a + 1, :] * float(PEER_NKEYS) + ids[1][0:counts[a], :] for a in range(PEER_TOPK)]
            + [jnp.zeros((npad, tb), F32)], axis=0)
        top_s, eidx = _topk_rows(cand, PEER_TOPK, payload=cidx)
        e = jnp.exp(top_s - jnp.max(top_s, axis=0, keepdims=True))
        gate = e / jnp.sum(e, axis=0, keepdims=True)
        gate_ref[hd * PEER_TOPK:(hd + 1) * PEER_TOPK, :] = gate
        eidx_ref[hd * PEER_TOPK:(hd + 1) * PEER_TOPK, :] = eidx.astype(I32)


def _select_call(layer, x1, mod, norm2, wq, keys, tb):
    b, t, d = x1.shape
    return pl.pallas_call(
        _select_kernel,
        grid=(b, t // tb),
        in_specs=[
            pl.BlockSpec((None, tb, d), lambda bi, ti: (bi, ti, 0)),
            pl.BlockSpec((None, 1, mod.shape[-1]), lambda bi, ti: (bi, 0, 0)),
            pl.BlockSpec((None, 1, d), lambda bi, ti: (layer, 0, 0)),
            pl.BlockSpec((None,) + wq.shape[1:], lambda bi, ti: (layer, 0, 0)),
            pl.BlockSpec((None,) + keys.shape[1:], lambda bi, ti: (layer, 0, 0, 0)),
        ],
        out_specs=[
            pl.BlockSpec((None, tb * (d // LANES), LANES), lambda bi, ti: (bi, ti, 0)),
            pl.BlockSpec((None, N_PAIRS, tb), lambda bi, ti: (bi, 0, ti)),
            pl.BlockSpec((None, N_PAIRS, tb), lambda bi, ti: (bi, 0, ti)),
        ],
        out_shape=[jax.ShapeDtypeStruct((b, t * (d // LANES), LANES), F32),
                   jax.ShapeDtypeStruct((b, N_PAIRS, t), F32),
                   jax.ShapeDtypeStruct((b, N_PAIRS, t), I32)],
        compiler_params=_params(("parallel", "parallel")),
        name="peer_select",
    )(x1, mod, norm2, wq, keys)


ROW_SLAB = 4
GROUP = 2 * SUBLANES
N_SLABS = 2


def _pack_table(tab):
    nl, ne, d = tab.shape
    bits = lax.bitcast_convert_type(tab.astype(BF16), jnp.uint16).astype(jnp.uint32)
    words = bits[:, :, :d // 2] | (bits[:, :, d // 2:] << 16)
    return lax.bitcast_convert_type(words, I32).reshape(nl, ne * ROW_SLAB, LANES)


def _gather_slabs(idx_ref, g, i, tab_ref, slab_ref):
    span = idx_ref.shape[0] - GROUP + 1
    for k in range(idx_ref.shape[1]):
        r = pl.multiple_of(idx_ref.at[pl.ds(i, span), pl.ds(k, 1)][g * GROUP, 0], ROW_SLAB)
        slab_ref[k * ROW_SLAB:(k + 1) * ROW_SLAB, :] = tab_ref[pl.ds(r, ROW_SLAB), :]


def _for_each_half(idx_hbm, bufs, sems, process):
    ts = bufs[0].shape[0]
    step = pl.program_id(0) * pl.num_programs(1) + pl.program_id(1)
    nstep = pl.num_programs(0) * pl.num_programs(1)

    def copy(blk, which):
        return pltpu.make_async_copy(idx_hbm.at[pl.ds(blk * ts, ts), :], bufs[which], sems.at[which])

    @pl.when(step == 0)
    def _():
        copy(0, 0).start()

    copy(2 * step + 1, 1).start()
    copy(2 * step, 0).wait()
    process(bufs[0], 0)

    @pl.when(step + 1 < nstep)
    def _():
        copy(2 * step + 2, 0).start()

    copy(2 * step + 1, 1).wait()
    process(bufs[1], ts)


def _peer_u_kernel(idx_hbm, x_ref, gate_ref, tab_ref, coef_ref, act_scr, idx_a, idx_b, sems, *slabs):
    ts, npair = idx_a.shape
    wide = 2 * npair
    kcat = ROW_SLAB * LANES
    rows2 = lax.broadcasted_iota(I32, (SUBLANES, wide), 0)
    lanes2 = lax.broadcasted_iota(I32, (SUBLANES, wide), 1)
    keep = (rows2 < ROW_SLAB) == ((lanes2 & 1) == 0)
    rowk = lax.broadcasted_iota(I32, (SUBLANES, kcat), 0)
    blk = lax.broadcasted_iota(I32, (SUBLANES, kcat), 1) >> 7
    xmask = (rowk == blk) | (rowk == blk + ROW_SLAB)

    def process(idx_ref, off):
        def group(g, carry):
            accs = [jnp.zeros((SUBLANES, wide), F32) for _ in range(GROUP // SUBLANES)]
            for i in range(GROUP):
                tl = g * GROUP + i
                slab = slabs[i % N_SLABS]
                _gather_slabs(idx_ref, g, i, tab_ref, slab)
                x8 = x_ref[pl.ds(pl.multiple_of((off + tl) * SUBLANES, SUBLANES), SUBLANES), :]
                lhs = jnp.where(xmask, jnp.concatenate([x8] * ROW_SLAB, axis=1), 0.0).astype(BF16)
                rhs = jnp.concatenate(
                    [pltpu.bitcast(slab[pl.ds(c, npair, stride=ROW_SLAB), :], BF16) for c in range(ROW_SLAB)],
                    axis=1)
                out = lax.dot_general(lhs, rhs, (((1,), (1,)), ((), ())), preferred_element_type=F32)
                z = jnp.sum(jnp.where(keep, out, 0.0), axis=0, keepdims=True)
                accs[i // SUBLANES] = jnp.where(rows2 == i % SUBLANES, z, accs[i // SUBLANES])
            for j, acc in enumerate(accs):
                act = acc + pltpu.roll(acc, wide - 1, 1)
                r0 = pl.multiple_of(off + g * GROUP + j * SUBLANES, SUBLANES)
                act_scr[pl.ds(r0, SUBLANES), :] = act
            return carry

        lax.fori_loop(0, ts // GROUP, group, 0)

    _for_each_half(idx_hbm, (idx_a, idx_b), sems, process)
    coef_ref[...] = gate_ref[...] * jax.nn.gelu(act_scr[...])


def _peer_u_call(idx4, h2v, gate2, tab, ts):
    b, t, npair = idx4.shape
    tu = 2 * ts
    wide = gate2.shape[-1]
    return pl.pallas_call(
        _peer_u_kernel,
        grid=(b, t // tu),
        in_specs=[
            pl.BlockSpec(memory_space=pl.ANY),
            pl.BlockSpec((None, tu * SUBLANES, LANES), lambda bi, ti: (bi, ti, 0)),
            pl.BlockSpec((None, tu, wide), lambda bi, ti: (bi, ti, 0)),
            pl.BlockSpec(tab.shape, lambda bi, ti: (0, 0), pipeline_mode=pl.Buffered(1)),
        ],
        out_specs=pl.BlockSpec((None, tu, wide), lambda bi, ti: (bi, ti, 0)),
        out_shape=jax.ShapeDtypeStruct((b, t, wide), F32),
        scratch_shapes=[pltpu.VMEM((tu, wide), F32),
                        pltpu.SMEM((ts, npair), I32),
                        pltpu.SMEM((ts, npair), I32),
                        pltpu.SemaphoreType.DMA((2,))]
        + [pltpu.VMEM((npair * ROW_SLAB, LANES), I32)] * N_SLABS,
        compiler_params=_params(("arbitrary", "arbitrary")),
        name="peer_u",
    )(idx4.reshape(b * t, npair), h2v, gate2, tab)


def _peer_v_kernel(idx_hbm, coef_ref, x_ref, mod_ref, tab_ref, o_ref, cx_scr, out_scr,
                   idx_a, idx_b, sems, *slabs):
    ts, npair = idx_a.shape
    tv, d = x_ref.shape
    per = 2 * ROW_SLAB
    nk = npair * per
    ek = lax.broadcasted_iota(I32, (2 * npair, nk), 0)
    ej = lax.broadcasted_iota(I32, (2 * npair, nk), 1)
    expand = jnp.where((ej >> 3) * 2 == ek, 1.0, 0.0).astype(BF16)
    cx_scr[...] = _dot(coef_ref[...], expand)
    mrow = lax.broadcasted_iota(I32, (SUBLANES, nk), 0)
    jj = lax.broadcasted_iota(I32, (SUBLANES, nk), 1) & (per - 1)
    place = mrow == (jj >> 1) + ROW_SLAB * (jj & 1)

    def process(idx_ref, off):
        def group(g, carry):
            cx8 = cx_scr[pl.ds(pl.multiple_of(off + g * GROUP, GROUP), GROUP), :]
            for i in range(GROUP):
                tl = g * GROUP + i
                slab = slabs[i % N_SLABS]
                _gather_slabs(idx_ref, g, i, tab_ref, slab)
                lhs = jnp.where(place, cx8[i:i + 1, :], 0.0).astype(BF16)
                out8 = jnp.dot(lhs, pltpu.bitcast(slab[...], BF16), preferred_element_type=F32)
                out_scr[pl.ds(pl.multiple_of((off + tl) * SUBLANES, SUBLANES), SUBLANES), :] = out8
            return carry

        lax.fori_loop(0, ts // GROUP, group, 0)

    _for_each_half(idx_hbm, (idx_a, idx_b), sems, process)
    g2 = mod_ref[:, 5 * d:6 * d]
    nchunk = d // LANES
    for c in range(nchunk):
        cs = slice(c * LANES, (c + 1) * LANES)
        o_ref[:, cs] = x_ref[:, cs] + g2[:, cs] * out_scr[pl.ds(c, tv, stride=nchunk), :]


def _peer_v_call(idx4, coef2, x1, mod, tab, ts):
    b, t, npair = idx4.shape
    d = x1.shape[-1]
    tv = 2 * ts
    return pl.pallas_call(
        _peer_v_kernel,
        grid=(b, t // tv),
        in_specs=[
            pl.BlockSpec(memory_space=pl.ANY),
            pl.BlockSpec((None, tv, 2 * npair), lambda bi, ti: (bi, ti, 0)),
            pl.BlockSpec((None, tv, d), lambda bi, ti: (bi, ti, 0)),
            pl.BlockSpec((None, 1, mod.shape[-1]), lambda bi, ti: (bi, 0, 0)),
            pl.BlockSpec(tab.shape, lambda bi, ti: (0, 0), pipeline_mode=pl.Buffered(1)),
        ],
        out_specs=pl.BlockSpec((None, tv, d), lambda bi, ti: (bi, ti, 0)),
        out_shape=jax.ShapeDtypeStruct(x1.shape, F32),
        scratch_shapes=[pltpu.VMEM((tv, npair * 2 * ROW_SLAB), F32),
                        pltpu.VMEM((tv * SUBLANES, LANES), F32),
                        pltpu.SMEM((ts, npair), I32),
                        pltpu.SMEM((ts, npair), I32),
                        pltpu.SemaphoreType.DMA((2,))]
        + [pltpu.VMEM((npair * ROW_SLAB, LANES), I32)] * N_SLABS,
        compiler_params=_params(("arbitrary", "arbitrary")),
        name="peer_v",
    )(idx4.reshape(b * t, npair), coef2, x1, mod, tab)


def _final_norm_kernel(x_ref, g_ref, o_ref):
    o_ref[...] = _rms(x_ref[...], g_ref[...])


def _final_norm_call(x, g, tb):
    b, t, d = x.shape
    return pl.pallas_call(
        _final_norm_kernel,
        grid=(b, t // tb),
        in_specs=[pl.BlockSpec((None, tb, d), lambda bi, ti: (bi, ti, 0)),
                  pl.BlockSpec((1, d), lambda bi, ti: (0, 0))],
        out_specs=pl.BlockSpec((None, tb, d), lambda bi, ti: (bi, ti, 0)),
        out_shape=jax.ShapeDtypeStruct((b, t, d), F32),
        compiler_params=_params(("parallel", "parallel")),
        name="final_norm",
    )(x, g.reshape(1, d))


def _block_diag(blocks):
    nl, g, r, c = blocks.shape
    eye = jnp.eye(g, dtype=blocks.dtype)
    return jnp.einsum("lgrc,gh->lgrhc", blocks, eye).reshape(nl, g * r, g * c)


def _chunk(t, pref):
    return pref if t % pref == 0 else t


def _run_group(x, mod, conv0, h0, s0r, s0i, w, tabs_u, tabs_v, norm_f):
    b, t, d = x.shape
    nl = mod.shape[0]
    tc = _chunk(t, 128)
    tb = _chunk(t, 128)
    ts = min(128, t // 2)
    convs, hs, srs, sis = [], [], [], []
    for l in range(nl):
        mod_l = mod[l].reshape(b, 1, mod.shape[-1])
        x1, nc, nh, nsr, nsi = _mixer_call(l, x, mod_l, conv0[l], h0[l], s0r[l], s0i[l], w, tc)
        convs.append(nc)
        hs.append(nh[:, 0])
        srs.append(nsr[:, 0])
        sis.append(nsi[:, 0])
        h2v, gate_t, eidx_t = _select_call(l, x1, mod_l, w["norm2"], w["peer_wq"], w["peer_keys"], tb)
        idx4 = jnp.transpose(eidx_t, (0, 2, 1)) * ROW_SLAB
        gate = jnp.transpose(gate_t, (0, 2, 1))
        gate2 = jnp.stack([gate, jnp.zeros_like(gate)], axis=-1).reshape(b, t, 2 * N_PAIRS)
        coef2 = _peer_u_call(idx4, h2v, gate2, tabs_u[l], ts)
        x = _peer_v_call(idx4, coef2, x1, mod_l, tabs_v[l], ts)
    y = _final_norm_call(x, norm_f, tb)
    return y, jnp.stack(convs), jnp.stack(hs), jnp.stack(srs), jnp.stack(sis)


def kernel(x_prompt, x_sample, cache_conv, state_lru, state_s5_re, state_s5_im, c_prompt, c_sample,
           w_ada, b_ada, norm1, norm2, w_in, conv_w, conv_b, lru_wa, lru_ba, lru_wx, lru_bx, lru_lam,
           s5_a_re, s5_a_im, s5_log_dt, s5_b_re, s5_b_im, s5_c_re, s5_c_im, s5_d, s5_w_glu, s5_b_glu,
           gn_lru, gn_s5, w_out, peer_wq, peer_keys, peer_u, peer_v, norm_f):
    nl, d, _ = w_ada.shape
    bp = x_prompt.shape[0]
    bs = x_sample.shape[0]
    dl = state_lru.shape[-1]
    g, p = s5_a_re.shape[1:]
    ns = g * p

    mod = _ada_call(jnp.concatenate([c_prompt, c_sample], axis=0), w_ada, b_ada)

    bt_re = jnp.transpose(s5_b_re, (0, 1, 3, 2))
    bt_im = jnp.transpose(s5_b_im, (0, 1, 3, 2))
    ab_re, ab_im, bb_re, bb_im = _s5_disc_call(s5_a_re, s5_a_im, s5_log_dt, bt_re, bt_im)
    b_full = jnp.concatenate([_block_diag(bb_re), _block_diag(bb_im)], axis=-1).astype(BF16)
    c_re = _block_diag(jnp.transpose(s5_c_re, (0, 1, 3, 2))).astype(BF16)
    c_im = _block_diag(jnp.transpose(s5_c_im, (0, 1, 3, 2))).astype(BF16)
    w_gate = jnp.concatenate([_block_diag(lru_wa), _block_diag(lru_wx)], axis=-1).astype(BF16)

    def vec(a):
        return a.reshape(nl, 1, a.shape[-1])

    w = {
        "norm1": vec(norm1), "w_in": w_in.astype(BF16), "conv_w": conv_w, "conv_b": vec(conv_b),
        "w_gate": w_gate, "b_gate": vec(jnp.concatenate([lru_ba, lru_bx], axis=-1)), "lam": vec(lru_lam),
        "ab_re": ab_re.reshape(nl, 1, ns), "ab_im": ab_im.reshape(nl, 1, ns),
        "b_full": b_full, "c_re": c_re, "c_im": c_im, "s5_d": vec(s5_d),
        "w_glu": s5_w_glu.astype(BF16), "b_glu": vec(s5_b_glu), "gn_lru": vec(gn_lru), "gn_s5": vec(gn_s5),
        "w_out": w_out.astype(BF16), "norm2": vec(norm2), "peer_wq": peer_wq.astype(BF16),
        "peer_keys": peer_keys.reshape(nl, 2 * PEER_HEADS, PEER_NKEYS, -1).astype(BF16),
    }
    tab_u = _pack_table(peer_u)
    tab_v = _pack_table(peer_v)

    zeros_conv = jnp.zeros((nl, bp, CONV_WIDTH - 1, dl), F32)
    zeros_lru = jnp.zeros((nl, bp, 1, dl), F32)
    zeros_s5 = jnp.zeros((nl, bp, 1, ns), F32)
    out_p = _run_group(x_prompt, mod[:, :bp], zeros_conv, zeros_lru, zeros_s5, zeros_s5,
                       w, tab_u, tab_v, norm_f)
    out_s = _run_group(x_sample, mod[:, bp:], cache_conv, state_lru.reshape(nl, bs, 1, dl),
                       state_s5_re.reshape(nl, bs, 1, ns), state_s5_im.reshape(nl, bs, 1, ns),
                       w, tab_u, tab_v, norm_f)

    def states(o, bsz):
        y, conv, hl, sr, si = o
        return y, conv, hl, sr.reshape(nl, bsz, g, p), si.reshape(nl, bsz, g, p)

    yp, cp, lp, rp, ip = states(out_p, bp)
    ys, cs, ls, rs, is_ = states(out_s, bs)
    return (yp, ys, cp, lp, rp, ip, cs, ls, rs, is_)
```

```python
import functools

import jax
import jax.numpy as jnp
from jax import lax
from jax.experimental import pallas as pl
from jax.experimental.pallas import tpu as pltpu

F32 = jnp.float32
BF16 = jnp.bfloat16
I32 = jnp.int32

EPS = 1e-6
LRU_C = 8.0
CONV_WIDTH = 4
N_LRU_HEADS = 8
S5_GROUP = 16
PEER_HEADS = 8
PEER_NKEYS = 128
PEER_TOPK = 16
N_PAIRS = PEER_HEADS * PEER_TOPK

LANES = 128
SUBLANES = 8
VMEM_LIMIT = 56 * 1024 * 1024

HI_MASK = -65536


def _rms(x, g):
    return x * lax.rsqrt(jnp.mean(x * x, axis=-1, keepdims=True) + EPS) * g


def _dot(a, b):
    return jnp.dot(a.astype(BF16), b, preferred_element_type=F32)


def _params(sem):
    return pltpu.CompilerParams(dimension_semantics=sem, vmem_limit_bytes=VMEM_LIMIT)


def _ada_kernel(c_ref, w_ref, b_ref, o_ref):
    c = c_ref[...]
    s = c * jax.nn.sigmoid(c)
    o_ref[...] = _dot(s, w_ref[...].astype(BF16)) + b_ref[...]


def _ada_call(c_all, w_ada, b_ada):
    nl, d, d6 = w_ada.shape
    nb = c_all.shape[0]
    bn = d
    return pl.pallas_call(
        _ada_kernel,
        grid=(nl, d6 // bn),
        in_specs=[
            pl.BlockSpec((nb, d), lambda l, j: (0, 0)),
            pl.BlockSpec((None, d, bn), lambda l, j: (l, 0, j)),
            pl.BlockSpec((None, 1, bn), lambda l, j: (l, 0, j)),
        ],
        out_specs=pl.BlockSpec((None, nb, bn), lambda l, j: (l, 0, j)),
        out_shape=jax.ShapeDtypeStruct((nl, nb, d6), F32),
        compiler_params=_params(("parallel", "parallel")),
        name="adaln_mod",
    )(c_all, w_ada, b_ada.reshape(nl, 1, d6))


def _s5_disc_kernel(are_ref, aim_ref, ldt_ref, bre_ref, bim_ref,
                    abre_ref, abim_ref, bbre_ref, bbim_ref):
    lr = are_ref[...]
    li = aim_ref[...]
    dt = jnp.exp(ldt_ref[...])
    mag = jnp.exp(lr * dt)
    ab_re = mag * jnp.cos(li * dt)
    ab_im = mag * jnp.sin(li * dt)
    den = lr * lr + li * li
    nr = ab_re - 1.0
    f_re = (nr * lr + ab_im * li) / den
    f_im = (ab_im * lr - nr * li) / den
    br = bre_ref[...]
    bi = bim_ref[...]
    abre_ref[...] = ab_re
    abim_ref[...] = ab_im
    bbre_ref[...] = f_re * br - f_im * bi
    bbim_ref[...] = f_re * bi + f_im * br


def _s5_disc_call(a_re, a_im, log_dt, bt_re, bt_im):
    nl, g, p = a_re.shape
    gi = bt_re.shape[2]
    spec_a = pl.BlockSpec((None, g, 1, p), lambda l: (l, 0, 0, 0))
    spec_b = pl.BlockSpec((None, g, gi, p), lambda l: (l, 0, 0, 0))
    return pl.pallas_call(
        _s5_disc_kernel,
        grid=(nl,),
        in_specs=[spec_a, spec_a, pl.BlockSpec((None, g, 1, 1), lambda l: (l, 0, 0, 0)), spec_b, spec_b],
        out_specs=[spec_a, spec_a, spec_b, spec_b],
        out_shape=[jax.ShapeDtypeStruct((nl, g, 1, p), F32)] * 2
        + [jax.ShapeDtypeStruct((nl, g, gi, p), F32)] * 2,
        compiler_params=_params(("parallel",)),
        name="s5_discretise",
    )(a_re.reshape(nl, g, 1, p), a_im.reshape(nl, g, 1, p), log_dt.reshape(nl, g, 1, 1), bt_re, bt_im)


SCAN_ROWS = 128


def _mixer_kernel(x_ref, mod_ref, n1_ref, win_ref, cw_ref, cb_ref, wg_ref, bg_ref, lam_ref,
                  are_ref, aim_ref, bblk_ref, cre_ref, cim_ref, d_ref, wglu_ref, bglu_ref,
                  gnl_ref, gns_ref, wout_ref, conv0_ref, h0_ref, s0r_ref, s0i_ref,
                  x1_ref, convo_ref, ho_ref, sro_ref, sio_ref,
                  xa_ext, hc, scr, sci, a_scr, b_scr, s_scr):
    tc, d = x_ref.shape
    dl = hc.shape[1]
    ns = scr.shape[1]
    nblk = wg_ref.shape[0]
    sw = ns // nblk
    tail = SUBLANES
    t = pl.program_id(1)

    @pl.when(t == 0)
    def _():
        xa_ext[0:tail, :] = jnp.zeros((tail, dl), F32)
        xa_ext[tail - (CONV_WIDTH - 1):tail, :] = conv0_ref[...]
        hc[...] = h0_ref[...]
        scr[...] = s0r_ref[...]
        sci[...] = s0i_ref[...]

    x = x_ref[...]
    mod = mod_ref[...]
    sh1, sc1, g1 = mod[:, 0:d], mod[:, d:2 * d], mod[:, 2 * d:3 * d]
    h = _rms(x, n1_ref[...]) * (1.0 + sc1) + sh1
    z = _dot(h, win_ref[...])
    xa = z[:, 0:dl]
    ga = z[:, dl:2 * dl]
    us = z[:, 2 * dl:]

    xa_ext[tail:tail + tc, :] = xa
    cw = cw_ref[...]
    xc = cb_ref[...] + cw[3:4, :] * xa
    for k in range(CONV_WIDTH - 1):
        back = CONV_WIDTH - 1 - k
        xc = xc + cw[k:k + 1, :] * xa_ext[tail - back:tail - back + tc, :]
    xa_ext[0:tail, :] = xa_ext[tc:tc + tail, :]
    convo_ref[...] = xa[tc - (CONV_WIDTH - 1):tc, :]

    nscan = min(tc, SCAN_ROWS)
    rows = lax.broadcasted_iota(I32, (nscan, LANES), 0)
    row0 = rows == 0

    xcb = xc.astype(BF16)
    gates = [jnp.dot(xcb[:, j * LANES:(j + 1) * LANES], wg_ref[j], preferred_element_type=F32)
             for j in range(nblk)]
    bg = bg_ref[...]
    r = jax.nn.sigmoid(jnp.concatenate([g[:, 0:LANES] for g in gates], axis=1) + bg[:, 0:dl])
    i = jax.nn.sigmoid(jnp.concatenate([g[:, LANES:] for g in gates], axis=1) + bg[:, dl:])
    log_a = -LRU_C * r * jax.nn.softplus(-lam_ref[...])
    a0 = jnp.exp(log_a)
    a_scr[...] = a0
    b_scr[...] = jnp.sqrt(-jnp.tanh(log_a) * (a0 * a0 + 1.0)) * (i * xc)

    def lru_block(j, carry):
        cs = pl.ds(pl.multiple_of(j * LANES, LANES), LANES)
        hprev = hc[:, cs]
        for r0 in range(0, tc, nscan):
            a = a_scr[r0:r0 + nscan, cs]
            bt = b_scr[r0:r0 + nscan, cs] + jnp.where(row0, a * hprev, 0.0)
            step = 1
            while step < nscan:
                m = rows >= step
                bt = jnp.where(m, a * pltpu.roll(bt, step, 0), 0.0) + bt
                if step * 2 < nscan:
                    a = jnp.where(m, a * pltpu.roll(a, step, 0), a)
                step *= 2
            b_scr[r0:r0 + nscan, cs] = bt
            hprev = bt[nscan - 1:nscan, :]
        return carry

    lax.fori_loop(0, dl // LANES, lru_block, 0)
    hl = b_scr[...]
    hc[...] = hl[tc - 1:tc, :]
    ho_ref[...] = hl[tc - 1:tc, :]
    y_lru = hl * jax.nn.gelu(ga)

    usb = us.astype(BF16)
    for j in range(nblk):
        bu = jnp.dot(usb[:, j * LANES:(j + 1) * LANES], bblk_ref[j], preferred_element_type=F32)
        s_scr[:, j * sw:(j + 1) * sw] = bu[:, 0:sw]
        s_scr[:, ns + j * sw:ns + (j + 1) * sw] = bu[:, sw:]

    def s5_block(j, carry):
        cr = pl.ds(pl.multiple_of(j * LANES, LANES), LANES)
        ci = pl.ds(pl.multiple_of(ns + j * LANES, LANES), LANES)
        c_r = scr[:, cr]
        c_i = sci[:, cr]
        for r0 in range(0, tc, nscan):
            pr = are_ref[:, cr]
            pi = aim_ref[:, cr]
            sr = s_scr[r0:r0 + nscan, cr] + jnp.where(row0, pr * c_r - pi * c_i, 0.0)
            si = s_scr[r0:r0 + nscan, ci] + jnp.where(row0, pr * c_i + pi * c_r, 0.0)
            step = 1
            while step < nscan:
                m = rows >= step
                qr = pltpu.roll(sr, step, 0)
                qi = pltpu.roll(si, step, 0)
                sr, si = (sr + jnp.where(m, pr * qr - pi * qi, 0.0),
                          si + jnp.where(m, pr * qi + pi * qr, 0.0))
                pr, pi = pr * pr - pi * pi, 2.0 * pr * pi
                step *= 2
            s_scr[r0:r0 + nscan, cr] = sr
            s_scr[r0:r0 + nscan, ci] = si
            c_r = sr[nscan - 1:nscan, :]
            c_i = si[nscan - 1:nscan, :]
        return carry

    lax.fori_loop(0, ns // LANES, s5_block, 0)
    sr = s_scr[:, 0:ns]
    si = s_scr[:, ns:]
    scr[...] = sr[tc - 1:tc, :]
    sci[...] = si[tc - 1:tc, :]
    sro_ref[...] = sr[tc - 1:tc, :]
    sio_ref[...] = si[tc - 1:tc, :]
    srb = sr.astype(BF16)
    sib = si.astype(BF16)
    ys = jnp.concatenate(
        [jnp.dot(srb[:, j * sw:(j + 1) * sw], cre_ref[j], preferred_element_type=F32)
         - jnp.dot(sib[:, j * sw:(j + 1) * sw], cim_ref[j], preferred_element_type=F32)
         for j in range(nblk)], axis=1) + d_ref[...] * us
    gy = jax.nn.gelu(ys)
    y_s5 = gy * jax.nn.sigmoid(_dot(gy, wglu_ref[...]) + bglu_ref[...])

    wout = wout_ref[...]
    mix = _dot(_rms(y_lru, gnl_ref[...]), wout[0:dl, :]) + _dot(_rms(y_s5, gns_ref[...]), wout[dl:, :])
    x1_ref[...] = x + g1 * mix


def _mixer_call(layer, x, mod, conv0, h0, s0r, s0i, w, tc):
    b, t, d = x.shape
    dl = h0.shape[-1]
    ns = s0r.shape[-1]

    def tok(shape):
        return pl.BlockSpec((None,) + shape, lambda bi, ti: (bi, ti, 0))

    def seq(shape):
        return pl.BlockSpec((None,) + shape, lambda bi, ti: (bi, 0, 0))

    def lay(arr):
        return pl.BlockSpec((None,) + arr.shape[1:], lambda bi, ti: (layer,) + (0,) * (arr.ndim - 1))

    names = ("norm1", "w_in", "conv_w", "conv_b", "w_gate", "b_gate", "lam", "ab_re", "ab_im",
             "b_blk", "c_re", "c_im", "s5_d", "w_glu", "b_glu", "gn_lru", "gn_s5", "w_out")
    ws = [w[n] for n in names]
    return pl.pallas_call(
        _mixer_kernel,
        grid=(b, t // tc),
        in_specs=[tok((tc, d)), seq((1, mod.shape[-1]))] + [lay(a) for a in ws]
        + [seq((CONV_WIDTH - 1, dl)), seq((1, dl)), seq((1, ns)), seq((1, ns))],
        out_specs=[tok((tc, d)), seq((CONV_WIDTH - 1, dl)), seq((1, dl)), seq((1, ns)), seq((1, ns))],
        out_shape=[jax.ShapeDtypeStruct((b, t, d), F32),
                   jax.ShapeDtypeStruct((b, CONV_WIDTH - 1, dl), F32),
                   jax.ShapeDtypeStruct((b, 1, dl), F32),
                   jax.ShapeDtypeStruct((b, 1, ns), F32),
                   jax.ShapeDtypeStruct((b, 1, ns), F32)],
        scratch_shapes=[pltpu.VMEM((tc + SUBLANES, dl), F32), pltpu.VMEM((1, dl), F32),
                        pltpu.VMEM((1, ns), F32), pltpu.VMEM((1, ns), F32),
                        pltpu.VMEM((tc, dl), F32), pltpu.VMEM((tc, dl), F32),
                        pltpu.VMEM((tc, 2 * ns), F32)],
        compiler_params=_params(("parallel", "arbitrary")),
        name="mixer",
    )(x, mod, *ws, conv0, h0, s0r, s0i)


def _topk_rows(s, k, payload=None):
    nrow = s.shape[0]
    rows = lax.broadcasted_iota(I32, s.shape, 0).astype(F32)
    vals, outs = [], []
    for _ in range(k):
        m = jnp.max(s, axis=0, keepdims=True)
        idx = jnp.min(jnp.where(s == m, rows, float(nrow)), axis=0, keepdims=True)
        hit = rows == idx
        vals.append(m)
        if payload is None:
            outs.append(idx)
        else:
            outs.append(jnp.max(jnp.where(hit, payload, -1.0), axis=0, keepdims=True))
        s = jnp.where(hit, -jnp.inf, s)
    return jnp.concatenate(vals, axis=0), jnp.concatenate(outs, axis=0)


def _select_kernel(x_ref, mod_ref, n2_ref, wq_ref, keys_ref, h2_ref, gate_ref, eidx_ref):
    tb, d = x_ref.shape
    x = x_ref[...]
    mod = mod_ref[...]
    sh2, sc2 = mod[:, 3 * d:4 * d], mod[:, 4 * d:5 * d]
    h2 = _rms(x, n2_ref[...]) * (1.0 + sc2) + sh2
    nchunk = d // LANES
    for c in range(nchunk):
        h2_ref[pl.ds(c, tb, stride=nchunk), :] = h2[:, c * LANES:(c + 1) * LANES]
    q = _dot(h2, wq_ref[...])
    counts = [PEER_TOPK // (a + 1) for a in range(PEER_TOPK)]
    npad = -sum(counts) % SUBLANES
    dk = keys_ref.shape[2]
    for hd in range(PEER_HEADS):
        vs, ids = [], []
        for c in range(2):
            j = 2 * hd + c
            qhc = q[:, j * dk:(j + 1) * dk].astype(BF16)
            st = lax.dot_general(keys_ref[j], qhc, (((1,), (1,)), ((), ())),
                                 preferred_element_type=F32)
            v, ix = _topk_rows(st, PEER_TOPK)
            vs.append(v)
            ids.append(ix)
        cand = jnp.concatenate(
            [vs[0][a:a + 1, :] + vs[1][0:counts[a], :] for a in range(PEER_TOPK)]
            + [jnp.full((npad, tb), -jnp.inf, F32)], axis=0)
        cidx = jnp.concatenate(
            [ids[0][a:a + 1, :] * float(PEER_NKEYS) + ids[1][0:counts[a], :] for a in range(PEER_TOPK)]
            + [jnp.zeros((npad, tb), F32)], axis=0)
        top_s, eidx = _topk_rows(cand, PEER_TOPK, payload=cidx)
        e = jnp.exp(top_s - jnp.max(top_s, axis=0, keepdims=True))
        gate = e / jnp.sum(e, axis=0, keepdims=True)
        gate_ref[hd * PEER_TOPK:(hd + 1) * PEER_TOPK, :] = gate
        eidx_ref[hd * PEER_TOPK:(hd + 1) * PEER_TOPK, :] = eidx.astype(I32)


def _select_call(layer, x1, mod, norm2, wq, keys, tb):
    b, t, d = x1.shape
    return pl.pallas_call(
        _select_kernel,
        grid=(b, t // tb),
        in_specs=[
            pl.BlockSpec((None, tb, d), lambda bi, ti: (bi, ti, 0)),
            pl.BlockSpec((None, 1, mod.shape[-1]), lambda bi, ti: (bi, 0, 0)),
            pl.BlockSpec((None, 1, d), lambda bi, ti: (layer, 0, 0)),
            pl.BlockSpec((None,) + wq.shape[1:], lambda bi, ti: (layer, 0, 0)),
            pl.BlockSpec((None,) + keys.shape[1:], lambda bi, ti: (layer, 0, 0, 0)),
        ],
        out_specs=[
            pl.BlockSpec((None, tb * (d // LANES), LANES), lambda bi, ti: (bi, ti, 0)),
            pl.BlockSpec((None, N_PAIRS, tb), lambda bi, ti: (bi, 0, ti)),
            pl.BlockSpec((None, N_PAIRS, tb), lambda bi, ti: (bi, 0, ti)),
        ],
        out_shape=[jax.ShapeDtypeStruct((b, t * (d // LANES), LANES), F32),
                   jax.ShapeDtypeStruct((b, N_PAIRS, t), F32),
                   jax.ShapeDtypeStruct((b, N_PAIRS, t), I32)],
        compiler_params=_params(("parallel", "parallel")),
        name="peer_select",
    )(x1, mod, norm2, wq, keys)


ROW_SLAB = 4
GROUP = 2 * SUBLANES
N_SLABS = 2


def _pack_table(tab):
    nl, ne, d = tab.shape
    bits = lax.bitcast_convert_type(tab.astype(BF16), jnp.uint16).astype(jnp.uint32)
    words = bits[:, :, :d // 2] | (bits[:, :, d // 2:] << 16)
    return lax.bitcast_convert_type(words, I32).reshape(nl, ne * ROW_SLAB, LANES)


def _gather_slabs(idx_ref, g, i, tab_ref, slab_ref):
    span = idx_ref.shape[0] - GROUP + 1
    for k in range(idx_ref.shape[1]):
        r = pl.multiple_of(idx_ref.at[pl.ds(i, span), pl.ds(k, 1)][g * GROUP, 0], ROW_SLAB)
        slab_ref[k * ROW_SLAB:(k + 1) * ROW_SLAB, :] = tab_ref[pl.ds(r, ROW_SLAB), :]


def _for_each_half(idx_hbm, bufs, sems, process):
    ts = bufs[0].shape[0]
    step = pl.program_id(0) * pl.num_programs(1) + pl.program_id(1)
    nstep = pl.num_programs(0) * pl.num_programs(1)

    def copy(blk, which):
        return pltpu.make_async_copy(idx_hbm.at[pl.ds(blk * ts, ts), :], bufs[which], sems.at[which])

    @pl.when(step == 0)
    def _():
        copy(0, 0).start()

    copy(2 * step + 1, 1).start()
    copy(2 * step, 0).wait()
    process(bufs[0], 0)

    @pl.when(step + 1 < nstep)
    def _():
        copy(2 * step + 2, 0).start()

    copy(2 * step + 1, 1).wait()
    process(bufs[1], ts)


def _peer_u_kernel(idx_hbm, x_ref, gate_ref, tab_ref, coef_ref, act_scr, idx_a, idx_b, sems, *slabs):
    ts, npair = idx_a.shape
    wide = 2 * npair
    kcat = ROW_SLAB * LANES
    rows2 = lax.broadcasted_iota(I32, (SUBLANES, wide), 0)
    lanes2 = lax.broadcasted_iota(I32, (SUBLANES, wide), 1)
    keep = (rows2 < ROW_SLAB) == ((lanes2 & 1) == 0)
    rowk = lax.broadcasted_iota(I32, (SUBLANES, kcat), 0)
    blk = lax.broadcasted_iota(I32, (SUBLANES, kcat), 1) >> 7
    xmask = (rowk == blk) | (rowk == blk + ROW_SLAB)

    def process(idx_ref, off):
        def group(g, carry):
            accs = [jnp.zeros((SUBLANES, wide), F32) for _ in range(GROUP // SUBLANES)]
            for i in range(GROUP):
                tl = g * GROUP + i
                slab = slabs[i % len(slabs)]
                _gather_slabs(idx_ref, g, i, tab_ref, slab)
                x8 = x_ref[pl.ds(pl.multiple_of((off + tl) * SUBLANES, SUBLANES), SUBLANES), :]
                lhs = jnp.where(xmask, jnp.concatenate([x8] * ROW_SLAB, axis=1), 0.0).astype(BF16)
                rhs = jnp.concatenate(
                    [pltpu.bitcast(slab[pl.ds(c, npair, stride=ROW_SLAB), :], BF16) for c in range(ROW_SLAB)],
                    axis=1)
                out = lax.dot_general(lhs, rhs, (((1,), (1,)), ((), ())), preferred_element_type=F32)
                z = jnp.sum(jnp.where(keep, out, 0.0), axis=0, keepdims=True)
                accs[i // SUBLANES] = jnp.where(rows2 == i % SUBLANES, z, accs[i // SUBLANES])
            for j, acc in enumerate(accs):
                act = acc + pltpu.roll(acc, wide - 1, 1)
                r0 = pl.multiple_of(off + g * GROUP + j * SUBLANES, SUBLANES)
                act_scr[pl.ds(r0, SUBLANES), :] = act
            return carry

        lax.fori_loop(0, ts // GROUP, group, 0)

    _for_each_half(idx_hbm, (idx_a, idx_b), sems, process)
    coef_ref[...] = gate_ref[...] * jax.nn.gelu(act_scr[...])


def _peer_u_call(idx4, h2v, gate2, tab, ts, n_slabs=N_SLABS):
    b, t, npair = idx4.shape
    tu = 2 * ts
    wide = gate2.shape[-1]
    return pl.pallas_call(
        _peer_u_kernel,
        grid=(b, t // tu),
        in_specs=[
            pl.BlockSpec(memory_space=pl.ANY),
            pl.BlockSpec((None, tu * SUBLANES, LANES), lambda bi, ti: (bi, ti, 0)),
            pl.BlockSpec((None, tu, wide), lambda bi, ti: (bi, ti, 0)),
            pl.BlockSpec(tab.shape, lambda bi, ti: (0, 0), pipeline_mode=pl.Buffered(1)),
        ],
        out_specs=pl.BlockSpec((None, tu, wide), lambda bi, ti: (bi, ti, 0)),
        out_shape=jax.ShapeDtypeStruct((b, t, wide), F32),
        scratch_shapes=[pltpu.VMEM((tu, wide), F32),
                        pltpu.SMEM((ts, npair), I32),
                        pltpu.SMEM((ts, npair), I32),
                        pltpu.SemaphoreType.DMA((2,))]
        + [pltpu.VMEM((npair * ROW_SLAB, LANES), I32)] * n_slabs,
        compiler_params=_params(("arbitrary", "arbitrary")),
        name="peer_u",
    )(idx4.reshape(b * t, npair), h2v, gate2, tab)


def _peer_v_kernel(final, idx_hbm, coef_ref, x_ref, mod_ref, nf_ref, tab_ref, o_ref, cx_scr, out_scr,
                   idx_a, idx_b, sems, *slabs):
    ts, npair = idx_a.shape
    tv, d = x_ref.shape
    per = 2 * ROW_SLAB
    nk = npair * per
    ek = lax.broadcasted_iota(I32, (2 * npair, nk), 0)
    ej = lax.broadcasted_iota(I32, (2 * npair, nk), 1)
    expand = jnp.where((ej >> 3) * 2 == ek, 1.0, 0.0).astype(BF16)
    cx_scr[...] = _dot(coef_ref[...], expand)
    mrow = lax.broadcasted_iota(I32, (SUBLANES, nk), 0)
    jj = lax.broadcasted_iota(I32, (SUBLANES, nk), 1) & (per - 1)
    place = mrow == (jj >> 1) + ROW_SLAB * (jj & 1)

    def process(idx_ref, off):
        def group(g, carry):
            cx8 = cx_scr[pl.ds(pl.multiple_of(off + g * GROUP, GROUP), GROUP), :]
            for i in range(GROUP):
                tl = g * GROUP + i
                slab = slabs[i % len(slabs)]
                _gather_slabs(idx_ref, g, i, tab_ref, slab)
                lhs = jnp.where(place, cx8[i:i + 1, :], 0.0).astype(BF16)
                out8 = jnp.dot(lhs, pltpu.bitcast(slab[...], BF16), preferred_element_type=F32)
                out_scr[pl.ds(pl.multiple_of((off + tl) * SUBLANES, SUBLANES), SUBLANES), :] = out8
            return carry

        lax.fori_loop(0, ts // GROUP, group, 0)

    _for_each_half(idx_hbm, (idx_a, idx_b), sems, process)
    g2 = mod_ref[:, 5 * d:6 * d]
    nchunk = d // LANES
    for c in range(nchunk):
        cs = slice(c * LANES, (c + 1) * LANES)
        o_ref[:, cs] = x_ref[:, cs] + g2[:, cs] * out_scr[pl.ds(c, tv, stride=nchunk), :]
    if final:
        o_ref[...] = _rms(o_ref[...], nf_ref[...])


def _peer_v_call(idx4, coef2, x1, mod, tab, norm_f, ts, final, n_slabs=N_SLABS):
    b, t, npair = idx4.shape
    d = x1.shape[-1]
    tv = 2 * ts
    return pl.pallas_call(
        functools.partial(_peer_v_kernel, final),
        grid=(b, t // tv),
        in_specs=[
            pl.BlockSpec(memory_space=pl.ANY),
            pl.BlockSpec((None, tv, 2 * npair), lambda bi, ti: (bi, ti, 0)),
            pl.BlockSpec((None, tv, d), lambda bi, ti: (bi, ti, 0)),
            pl.BlockSpec((None, 1, mod.shape[-1]), lambda bi, ti: (bi, 0, 0)),
            pl.BlockSpec((1, d), lambda bi, ti: (0, 0)),
            pl.BlockSpec(tab.shape, lambda bi, ti: (0, 0), pipeline_mode=pl.Buffered(1)),
        ],
        out_specs=pl.BlockSpec((None, tv, d), lambda bi, ti: (bi, ti, 0)),
        out_shape=jax.ShapeDtypeStruct(x1.shape, F32),
        scratch_shapes=[pltpu.VMEM((tv, npair * 2 * ROW_SLAB), F32),
                        pltpu.VMEM((tv * SUBLANES, LANES), F32),
                        pltpu.SMEM((ts, npair), I32),
                        pltpu.SMEM((ts, npair), I32),
                        pltpu.SemaphoreType.DMA((2,))]
        + [pltpu.VMEM((npair * ROW_SLAB, LANES), I32)] * n_slabs,
        compiler_params=_params(("arbitrary", "arbitrary")),
        name="peer_v",
    )(idx4.reshape(b * t, npair), coef2, x1, mod, norm_f.reshape(1, d), tab)


def _block_diag(blocks):
    nl, g, r, c = blocks.shape
    eye = jnp.eye(g, dtype=blocks.dtype)
    return jnp.einsum("lgrc,gh->lgrhc", blocks, eye).reshape(nl, g * r, g * c)


def _chunk(t, pref):
    return pref if t % pref == 0 else t


def _run_group(x, mod, conv0, h0, s0r, s0i, w, tabs_u, tabs_v, norm_f):
    b, t, d = x.shape
    nl = mod.shape[0]
    tc = _chunk(t, 256)
    tb = _chunk(t, 128)
    ts = min(128, t // 2)
    convs, hs, srs, sis = [], [], [], []
    for l in range(nl):
        mod_l = mod[l].reshape(b, 1, mod.shape[-1])
        x1, nc, nh, nsr, nsi = _mixer_call(l, x, mod_l, conv0[l], h0[l], s0r[l], s0i[l], w, tc)
        convs.append(nc)
        hs.append(nh[:, 0])
        srs.append(nsr[:, 0])
        sis.append(nsi[:, 0])
        h2v, gate_t, eidx_t = _select_call(l, x1, mod_l, w["norm2"], w["peer_wq"], w["peer_keys"], tb)
        idx4 = jnp.transpose(eidx_t, (0, 2, 1)) * ROW_SLAB
        gate = jnp.transpose(gate_t, (0, 2, 1))
        gate2 = jnp.stack([gate, jnp.zeros_like(gate)], axis=-1).reshape(b, t, 2 * N_PAIRS)
        coef2 = _peer_u_call(idx4, h2v, gate2, tabs_u[l], ts)
        x = _peer_v_call(idx4, coef2, x1, mod_l, tabs_v[l], norm_f, ts, final=(l == nl - 1))
    return x, jnp.stack(convs), jnp.stack(hs), jnp.stack(srs), jnp.stack(sis)


def kernel(x_prompt, x_sample, cache_conv, state_lru, state_s5_re, state_s5_im, c_prompt, c_sample,
           w_ada, b_ada, norm1, norm2, w_in, conv_w, conv_b, lru_wa, lru_ba, lru_wx, lru_bx, lru_lam,
           s5_a_re, s5_a_im, s5_log_dt, s5_b_re, s5_b_im, s5_c_re, s5_c_im, s5_d, s5_w_glu, s5_b_glu,
           gn_lru, gn_s5, w_out, peer_wq, peer_keys, peer_u, peer_v, norm_f):
    nl, d, _ = w_ada.shape
    bp = x_prompt.shape[0]
    bs = x_sample.shape[0]
    dl = state_lru.shape[-1]
    g, p = s5_a_re.shape[1:]
    ns = g * p

    mod = _ada_call(jnp.concatenate([c_prompt, c_sample], axis=0), w_ada, b_ada)

    bt_re = jnp.transpose(s5_b_re, (0, 1, 3, 2))
    bt_im = jnp.transpose(s5_b_im, (0, 1, 3, 2))
    ab_re, ab_im, bb_re, bb_im = _s5_disc_call(s5_a_re, s5_a_im, s5_log_dt, bt_re, bt_im)
    nblk = dl // LANES

    def blocks(a):
        per = a.shape[1] // nblk
        out = _block_diag(a.reshape((nl * nblk, per) + a.shape[2:]))
        return out.reshape((nl, nblk) + out.shape[1:]).astype(BF16)

    b_blk = jnp.concatenate([blocks(bb_re), blocks(bb_im)], axis=-1)
    c_re = blocks(jnp.transpose(s5_c_re, (0, 1, 3, 2)))
    c_im = blocks(jnp.transpose(s5_c_im, (0, 1, 3, 2)))
    w_gate = jnp.concatenate([blocks(lru_wa), blocks(lru_wx)], axis=-1)

    def vec(a):
        return a.reshape(nl, 1, a.shape[-1])

    w = {
        "norm1": vec(norm1), "w_in": w_in.astype(BF16), "conv_w": conv_w, "conv_b": vec(conv_b),
        "w_gate": w_gate, "b_gate": vec(jnp.concatenate([lru_ba, lru_bx], axis=-1)), "lam": vec(lru_lam),
        "ab_re": ab_re.reshape(nl, 1, ns), "ab_im": ab_im.reshape(nl, 1, ns),
        "b_blk": b_blk, "c_re": c_re, "c_im": c_im, "s5_d": vec(s5_d),
        "w_glu": s5_w_glu.astype(BF16), "b_glu": vec(s5_b_glu), "gn_lru": vec(gn_lru), "gn_s5": vec(gn_s5),
        "w_out": w_out.astype(BF16), "norm2": vec(norm2), "peer_wq": peer_wq.astype(BF16),
        "peer_keys": peer_keys.reshape(nl, 2 * PEER_HEADS, PEER_NKEYS, -1).astype(BF16),
    }
    tab_u = _pack_table(peer_u)
    tab_v = _pack_table(peer_v)

    zeros_conv = jnp.zeros((nl, bp, CONV_WIDTH - 1, dl), F32)
    zeros_lru = jnp.zeros((nl, bp, 1, dl), F32)
    zeros_s5 = jnp.zeros((nl, bp, 1, ns), F32)
    out_p = _run_group(x_prompt, mod[:, :bp], zeros_conv, zeros_lru, zeros_s5, zeros_s5,
                       w, tab_u, tab_v, norm_f)
    out_s = _run_group(x_sample, mod[:, bp:], cache_conv, state_lru.reshape(nl, bs, 1, dl),
                       state_s5_re.reshape(nl, bs, 1, ns), state_s5_im.reshape(nl, bs, 1, ns),
                       w, tab_u, tab_v, norm_f)

    def states(o, bsz):
        y, conv, hl, sr, si = o
        return y, conv, hl, sr.reshape(nl, bsz, g, p), si.reshape(nl, bsz, g, p)

    yp, cp, lp, rp, ip = states(out_p, bp)
    ys, cs, ls, rs, is_ = states(out_s, bs)
    return (yp, ys, cp, lp, rp, ip, cs, ls, rs, is_)
```

```python
import functools

import jax
import jax.numpy as jnp
from jax import lax
from jax.experimental import pallas as pl
from jax.experimental.pallas import tpu as pltpu

F32 = jnp.float32
BF16 = jnp.bfloat16
I32 = jnp.int32

EPS = 1e-6
LRU_C = 8.0
CONV_WIDTH = 4
N_LRU_HEADS = 8
S5_GROUP = 16
PEER_HEADS = 8
PEER_NKEYS = 128
PEER_TOPK = 16
N_PAIRS = PEER_HEADS * PEER_TOPK

LANES = 128
SUBLANES = 8
VMEM_LIMIT = 56 * 1024 * 1024

HI_MASK = -65536


def _rms(x, g):
    return x * lax.rsqrt(jnp.mean(x * x, axis=-1, keepdims=True) + EPS) * g


def _dot(a, b):
    return jnp.dot(a.astype(BF16), b, preferred_element_type=F32)


def _params(sem):
    return pltpu.CompilerParams(dimension_semantics=sem, vmem_limit_bytes=VMEM_LIMIT)


def _ada_kernel(c_ref, w_ref, b_ref, o_ref):
    c = c_ref[...]
    s = c * jax.nn.sigmoid(c)
    o_ref[...] = _dot(s, w_ref[...].astype(BF16)) + b_ref[...]


def _ada_call(c_all, w_ada, b_ada):
    nl, d, d6 = w_ada.shape
    nb = c_all.shape[0]
    bn = d
    return pl.pallas_call(
        _ada_kernel,
        grid=(nl, d6 // bn),
        in_specs=[
            pl.BlockSpec((nb, d), lambda l, j: (0, 0)),
            pl.BlockSpec((None, d, bn), lambda l, j: (l, 0, j)),
            pl.BlockSpec((None, 1, bn), lambda l, j: (l, 0, j)),
        ],
        out_specs=pl.BlockSpec((None, nb, bn), lambda l, j: (l, 0, j)),
        out_shape=jax.ShapeDtypeStruct((nl, nb, d6), F32),
        compiler_params=_params(("parallel", "parallel")),
        name="adaln_mod",
    )(c_all, w_ada, b_ada.reshape(nl, 1, d6))


def _s5_disc_kernel(are_ref, aim_ref, ldt_ref, bre_ref, bim_ref,
                    abre_ref, abim_ref, bbre_ref, bbim_ref):
    lr = are_ref[...]
    li = aim_ref[...]
    dt = jnp.exp(ldt_ref[...])
    mag = jnp.exp(lr * dt)
    ab_re = mag * jnp.cos(li * dt)
    ab_im = mag * jnp.sin(li * dt)
    den = lr * lr + li * li
    nr = ab_re - 1.0
    f_re = (nr * lr + ab_im * li) / den
    f_im = (ab_im * lr - nr * li) / den
    br = bre_ref[...]
    bi = bim_ref[...]
    abre_ref[...] = ab_re
    abim_ref[...] = ab_im
    bbre_ref[...] = f_re * br - f_im * bi
    bbim_ref[...] = f_re * bi + f_im * br


def _s5_disc_call(a_re, a_im, log_dt, bt_re, bt_im):
    nl, g, p = a_re.shape
    gi = bt_re.shape[2]
    spec_a = pl.BlockSpec((None, g, 1, p), lambda l: (l, 0, 0, 0))
    spec_b = pl.BlockSpec((None, g, gi, p), lambda l: (l, 0, 0, 0))
    return pl.pallas_call(
        _s5_disc_kernel,
        grid=(nl,),
        in_specs=[spec_a, spec_a, pl.BlockSpec((None, g, 1, 1), lambda l: (l, 0, 0, 0)), spec_b, spec_b],
        out_specs=[spec_a, spec_a, spec_b, spec_b],
        out_shape=[jax.ShapeDtypeStruct((nl, g, 1, p), F32)] * 2
        + [jax.ShapeDtypeStruct((nl, g, gi, p), F32)] * 2,
        compiler_params=_params(("parallel",)),
        name="s5_discretise",
    )(a_re.reshape(nl, g, 1, p), a_im.reshape(nl, g, 1, p), log_dt.reshape(nl, g, 1, 1), bt_re, bt_im)


SCAN_ROWS = 128


def _mixer_kernel(x_ref, mod_ref, n1_ref, win_ref, cw_ref, cb_ref, wg_ref, bg_ref, lam_ref,
                  are_ref, aim_ref, bblk_ref, cre_ref, cim_ref, d_ref, wglu_ref, bglu_ref,
                  gnl_ref, gns_ref, wout_ref, conv0_ref, h0_ref, s0r_ref, s0i_ref,
                  x1_ref, convo_ref, ho_ref, sro_ref, sio_ref,
                  xa_ext, hc, scr, sci, a_scr, b_scr, s_scr):
    tc, d = x_ref.shape
    dl = hc.shape[1]
    ns = scr.shape[1]
    nblk = wg_ref.shape[0]
    sw = ns // nblk
    tail = SUBLANES
    t = pl.program_id(1)

    @pl.when(t == 0)
    def _():
        xa_ext[0:tail, :] = jnp.zeros((tail, dl), F32)
        xa_ext[tail - (CONV_WIDTH - 1):tail, :] = conv0_ref[...]
        hc[...] = h0_ref[...]
        scr[...] = s0r_ref[...]
        sci[...] = s0i_ref[...]

    x = x_ref[...]
    mod = mod_ref[...]
    sh1, sc1, g1 = mod[:, 0:d], mod[:, d:2 * d], mod[:, 2 * d:3 * d]
    h = _rms(x, n1_ref[...]) * (1.0 + sc1) + sh1
    z = _dot(h, win_ref[...])
    xa = z[:, 0:dl]
    ga = z[:, dl:2 * dl]
    us = z[:, 2 * dl:]

    xa_ext[tail:tail + tc, :] = xa
    cw = cw_ref[...]
    xc = cb_ref[...] + cw[3:4, :] * xa
    for k in range(CONV_WIDTH - 1):
        back = CONV_WIDTH - 1 - k
        xc = xc + cw[k:k + 1, :] * xa_ext[tail - back:tail - back + tc, :]
    xa_ext[0:tail, :] = xa_ext[tc:tc + tail, :]
    convo_ref[...] = xa[tc - (CONV_WIDTH - 1):tc, :]

    nscan = min(tc, SCAN_ROWS)
    rows = lax.broadcasted_iota(I32, (nscan, LANES), 0)
    row0 = rows == 0

    xcb = xc.astype(BF16)
    gates = [jnp.dot(xcb[:, j * LANES:(j + 1) * LANES], wg_ref[j], preferred_element_type=F32)
             for j in range(nblk)]
    bg = bg_ref[...]
    r = jax.nn.sigmoid(jnp.concatenate([g[:, 0:LANES] for g in gates], axis=1) + bg[:, 0:dl])
    i = jax.nn.sigmoid(jnp.concatenate([g[:, LANES:] for g in gates], axis=1) + bg[:, dl:])
    log_a = -LRU_C * r * jax.nn.softplus(-lam_ref[...])
    a0 = jnp.exp(log_a)
    a_scr[...] = a0
    b_scr[...] = jnp.sqrt(-jnp.tanh(log_a) * (a0 * a0 + 1.0)) * (i * xc)

    def lru_block(j, carry):
        cs = pl.ds(pl.multiple_of(j * LANES, LANES), LANES)
        hprev = hc[:, cs]
        for r0 in range(0, tc, nscan):
            a = a_scr[r0:r0 + nscan, cs]
            bt = b_scr[r0:r0 + nscan, cs] + jnp.where(row0, a * hprev, 0.0)
            step = 1
            while step < nscan:
                m = rows >= step
                bt = jnp.where(m, a * pltpu.roll(bt, step, 0), 0.0) + bt
                if step * 2 < nscan:
                    a = jnp.where(m, a * pltpu.roll(a, step, 0), a)
                step *= 2
            b_scr[r0:r0 + nscan, cs] = bt
            hprev = bt[nscan - 1:nscan, :]
        return carry

    lax.fori_loop(0, dl // LANES, lru_block, 0)
    hl = b_scr[...]
    hc[...] = hl[tc - 1:tc, :]
    ho_ref[...] = hl[tc - 1:tc, :]
    y_lru = hl * jax.nn.gelu(ga)

    usb = us.astype(BF16)
    for j in range(nblk):
        bu = jnp.dot(usb[:, j * LANES:(j + 1) * LANES], bblk_ref[j], preferred_element_type=F32)
        s_scr[:, j * sw:(j + 1) * sw] = bu[:, 0:sw]
        s_scr[:, ns + j * sw:ns + (j + 1) * sw] = bu[:, sw:]

    def s5_block(j, carry):
        cr = pl.ds(pl.multiple_of(j * LANES, LANES), LANES)
        ci = pl.ds(pl.multiple_of(ns + j * LANES, LANES), LANES)
        c_r = scr[:, cr]
        c_i = sci[:, cr]
        for r0 in range(0, tc, nscan):
            pr = are_ref[:, cr]
            pi = aim_ref[:, cr]
            sr = s_scr[r0:r0 + nscan, cr] + jnp.where(row0, pr * c_r - pi * c_i, 0.0)
            si = s_scr[r0:r0 + nscan, ci] + jnp.where(row0, pr * c_i + pi * c_r, 0.0)
            step = 1
            while step < nscan:
                m = rows >= step
                qr = pltpu.roll(sr, step, 0)
                qi = pltpu.roll(si, step, 0)
                sr, si = (sr + jnp.where(m, pr * qr - pi * qi, 0.0),
                          si + jnp.where(m, pr * qi + pi * qr, 0.0))
                pr, pi = pr * pr - pi * pi, 2.0 * pr * pi
                step *= 2
            s_scr[r0:r0 + nscan, cr] = sr
            s_scr[r0:r0 + nscan, ci] = si
            c_r = sr[nscan - 1:nscan, :]
            c_i = si[nscan - 1:nscan, :]
        return carry

    lax.fori_loop(0, ns // LANES, s5_block, 0)
    sr = s_scr[:, 0:ns]
    si = s_scr[:, ns:]
    scr[...] = sr[tc - 1:tc, :]
    sci[...] = si[tc - 1:tc, :]
    sro_ref[...] = sr[tc - 1:tc, :]
    sio_ref[...] = si[tc - 1:tc, :]
    srb = sr.astype(BF16)
    sib = si.astype(BF16)
    ys = jnp.concatenate(
        [jnp.dot(srb[:, j * sw:(j + 1) * sw], cre_ref[j], preferred_element_type=F32)
         - jnp.dot(sib[:, j * sw:(j + 1) * sw], cim_ref[j], preferred_element_type=F32)
         for j in range(nblk)], axis=1) + d_ref[...] * us
    gy = jax.nn.gelu(ys)
    y_s5 = gy * jax.nn.sigmoid(_dot(gy, wglu_ref[...]) + bglu_ref[...])

    wout = wout_ref[...]
    mix = _dot(_rms(y_lru, gnl_ref[...]), wout[0:dl, :]) + _dot(_rms(y_s5, gns_ref[...]), wout[dl:, :])
    x1_ref[...] = x + g1 * mix


def _mixer_call(layer, x, mod, conv0, h0, s0r, s0i, w, tc):
    b, t, d = x.shape
    dl = h0.shape[-1]
    ns = s0r.shape[-1]

    def tok(shape):
        return pl.BlockSpec((None,) + shape, lambda bi, ti: (bi, ti, 0))

    def seq(shape):
        return pl.BlockSpec((None,) + shape, lambda bi, ti: (bi, 0, 0))

    def lay(arr):
        return pl.BlockSpec((None,) + arr.shape[1:], lambda bi, ti: (layer,) + (0,) * (arr.ndim - 1))

    names = ("norm1", "w_in", "conv_w", "conv_b", "w_gate", "b_gate", "lam", "ab_re", "ab_im",
             "b_blk", "c_re", "c_im", "s5_d", "w_glu", "b_glu", "gn_lru", "gn_s5", "w_out")
    ws = [w[n] for n in names]
    return pl.pallas_call(
        _mixer_kernel,
        grid=(b, t // tc),
        in_specs=[tok((tc, d)), seq((1, mod.shape[-1]))] + [lay(a) for a in ws]
        + [seq((CONV_WIDTH - 1, dl)), seq((1, dl)), seq((1, ns)), seq((1, ns))],
        out_specs=[tok((tc, d)), seq((CONV_WIDTH - 1, dl)), seq((1, dl)), seq((1, ns)), seq((1, ns))],
        out_shape=[jax.ShapeDtypeStruct((b, t, d), F32),
                   jax.ShapeDtypeStruct((b, CONV_WIDTH - 1, dl), F32),
                   jax.ShapeDtypeStruct((b, 1, dl), F32),
                   jax.ShapeDtypeStruct((b, 1, ns), F32),
                   jax.ShapeDtypeStruct((b, 1, ns), F32)],
        scratch_shapes=[pltpu.VMEM((tc + SUBLANES, dl), F32), pltpu.VMEM((1, dl), F32),
                        pltpu.VMEM((1, ns), F32), pltpu.VMEM((1, ns), F32),
                        pltpu.VMEM((tc, dl), F32), pltpu.VMEM((tc, dl), F32),
                        pltpu.VMEM((tc, 2 * ns), F32)],
        compiler_params=_params(("parallel", "arbitrary")),
        name="mixer",
    )(x, mod, *ws, conv0, h0, s0r, s0i)


def _topk_rows(s, k, payload=None):
    nrow = s.shape[0]
    rows = lax.broadcasted_iota(I32, s.shape, 0).astype(F32)
    vals, outs = [], []
    for _ in range(k):
        m = jnp.max(s, axis=0, keepdims=True)
        idx = jnp.min(jnp.where(s == m, rows, float(nrow)), axis=0, keepdims=True)
        hit = rows == idx
        vals.append(m)
        if payload is None:
            outs.append(idx)
        else:
            outs.append(jnp.max(jnp.where(hit, payload, -1.0), axis=0, keepdims=True))
        s = jnp.where(hit, -jnp.inf, s)
    return jnp.concatenate(vals, axis=0), jnp.concatenate(outs, axis=0)


def _select_kernel(x_ref, mod_ref, n2_ref, wq_ref, keys_ref, h2_ref, gate_ref, eidx_ref):
    tb, d = x_ref.shape
    x = x_ref[...]
    mod = mod_ref[...]
    sh2, sc2 = mod[:, 3 * d:4 * d], mod[:, 4 * d:5 * d]
    h2 = _rms(x, n2_ref[...]) * (1.0 + sc2) + sh2
    nchunk = d // LANES
    for c in range(nchunk):
        h2_ref[pl.ds(c, tb, stride=nchunk), :] = h2[:, c * LANES:(c + 1) * LANES]
    q = _dot(h2, wq_ref[...])
    counts = [PEER_TOPK // (a + 1) for a in range(PEER_TOPK)]
    npad = -sum(counts) % SUBLANES
    dk = keys_ref.shape[2]
    for hd in range(PEER_HEADS):
        vs, ids = [], []
        for c in range(2):
            j = 2 * hd + c
            qhc = q[:, j * dk:(j + 1) * dk].astype(BF16)
            st = lax.dot_general(keys_ref[j], qhc, (((1,), (1,)), ((), ())),
                                 preferred_element_type=F32)
            v, ix = _topk_rows(st, PEER_TOPK)
            vs.append(v)
            ids.append(ix)
        cand = jnp.concatenate(
            [vs[0][a:a + 1, :] + vs[1][0:counts[a], :] for a in range(PEER_TOPK)]
            + [jnp.full((npad, tb), -jnp.inf, F32)], axis=0)
        cidx = jnp.concatenate(
            [ids[0][a:a + 1, :] * float(PEER_NKEYS) + ids[1][0:counts[a], :] for a in range(PEER_TOPK)]
            + [jnp.zeros((npad, tb), F32)], axis=0)
        top_s, eidx = _topk_rows(cand, PEER_TOPK, payload=cidx)
        e = jnp.exp(top_s - jnp.max(top_s, axis=0, keepdims=True))
        gate = e / jnp.sum(e, axis=0, keepdims=True)
        gate_ref[hd * PEER_TOPK:(hd + 1) * PEER_TOPK, :] = gate
        eidx_ref[hd * PEER_TOPK:(hd + 1) * PEER_TOPK, :] = eidx.astype(I32)


def _select_call(layer, x1, mod, norm2, wq, keys, tb):
    b, t, d = x1.shape
    return pl.pallas_call(
        _select_kernel,
        grid=(b, t // tb),
        in_specs=[
            pl.BlockSpec((None, tb, d), lambda bi, ti: (bi, ti, 0)),
            pl.BlockSpec((None, 1, mod.shape[-1]), lambda bi, ti: (bi, 0, 0)),
            pl.BlockSpec((None, 1, d), lambda bi, ti: (layer, 0, 0)),
            pl.BlockSpec((None,) + wq.shape[1:], lambda bi, ti: (layer, 0, 0)),
            pl.BlockSpec((None,) + keys.shape[1:], lambda bi, ti: (layer, 0, 0, 0)),
        ],
        out_specs=[
            pl.BlockSpec((None, tb * (d // LANES), LANES), lambda bi, ti: (bi, ti, 0)),
            pl.BlockSpec((None, N_PAIRS, tb), lambda bi, ti: (bi, 0, ti)),
            pl.BlockSpec((None, N_PAIRS, tb), lambda bi, ti: (bi, 0, ti)),
        ],
        out_shape=[jax.ShapeDtypeStruct((b, t * (d // LANES), LANES), F32),
                   jax.ShapeDtypeStruct((b, N_PAIRS, t), F32),
                   jax.ShapeDtypeStruct((b, N_PAIRS, t), I32)],
        compiler_params=_params(("parallel", "parallel")),
        name="peer_select",
    )(x1, mod, norm2, wq, keys)


ROW_SLAB = 4
GROUP = 4 * SUBLANES
N_SLABS_U = 2
N_SLABS_V = 4


def _pack_table(tab):
    nl, ne, d = tab.shape
    bits = lax.bitcast_convert_type(tab.astype(BF16), jnp.uint16).astype(jnp.uint32)
    words = bits[:, :, :d // 2] | (bits[:, :, d // 2:] << 16)
    return lax.bitcast_convert_type(words, I32).reshape(nl, ne * ROW_SLAB, LANES)


def _gather_slabs(idx_ref, g, i, ng, tab_ref, slab_ref):
    span = idx_ref.shape[0] - ng + 1
    for k in range(idx_ref.shape[1]):
        r = pl.multiple_of(idx_ref.at[pl.ds(i, span), pl.ds(k, 1)][g * ng, 0], ROW_SLAB)
        slab_ref[k * ROW_SLAB:(k + 1) * ROW_SLAB, :] = tab_ref[pl.ds(r, ROW_SLAB), :]


def _for_each_half(idx_hbm, bufs, sems, process):
    ts = bufs[0].shape[0]
    step = pl.program_id(0) * pl.num_programs(1) + pl.program_id(1)
    nstep = pl.num_programs(0) * pl.num_programs(1)

    def copy(blk, which):
        return pltpu.make_async_copy(idx_hbm.at[pl.ds(blk * ts, ts), :], bufs[which], sems.at[which])

    @pl.when(step == 0)
    def _():
        copy(0, 0).start()

    copy(2 * step + 1, 1).start()
    copy(2 * step, 0).wait()
    process(bufs[0], 0)

    @pl.when(step + 1 < nstep)
    def _():
        copy(2 * step + 2, 0).start()

    copy(2 * step + 1, 1).wait()
    process(bufs[1], ts)


def _peer_u_kernel(ng, idx_hbm, x_ref, gate_ref, tab_ref, coef_ref, act_scr, idx_a, idx_b, sems, *slabs):
    ts, npair = idx_a.shape
    wide = 2 * npair
    kcat = ROW_SLAB * LANES
    rows2 = lax.broadcasted_iota(I32, (SUBLANES, wide), 0)
    lanes2 = lax.broadcasted_iota(I32, (SUBLANES, wide), 1)
    keep = (rows2 < ROW_SLAB) == ((lanes2 & 1) == 0)
    rowk = lax.broadcasted_iota(I32, (SUBLANES, kcat), 0)
    blk = lax.broadcasted_iota(I32, (SUBLANES, kcat), 1) >> 7
    xmask = (rowk == blk) | (rowk == blk + ROW_SLAB)

    def process(idx_ref, off):
        def group(g, carry):
            accs = [jnp.zeros((SUBLANES, wide), F32) for _ in range(ng // SUBLANES)]
            for i in range(ng):
                tl = g * ng + i
                slab = slabs[i % len(slabs)]
                _gather_slabs(idx_ref, g, i, ng, tab_ref, slab)
                x8 = x_ref[pl.ds(pl.multiple_of((off + tl) * SUBLANES, SUBLANES), SUBLANES), :]
                lhs = jnp.where(xmask, jnp.concatenate([x8] * ROW_SLAB, axis=1), 0.0).astype(BF16)
                rhs = jnp.concatenate(
                    [pltpu.bitcast(slab[pl.ds(c, npair, stride=ROW_SLAB), :], BF16) for c in range(ROW_SLAB)],
                    axis=1)
                out = lax.dot_general(lhs, rhs, (((1,), (1,)), ((), ())), preferred_element_type=F32)
                z = jnp.sum(jnp.where(keep, out, 0.0), axis=0, keepdims=True)
                accs[i // SUBLANES] = jnp.where(rows2 == i % SUBLANES, z, accs[i // SUBLANES])
            for j, acc in enumerate(accs):
                act = acc + pltpu.roll(acc, wide - 1, 1)
                r0 = pl.multiple_of(off + g * ng + j * SUBLANES, SUBLANES)
                act_scr[pl.ds(r0, SUBLANES), :] = act
            return carry

        lax.fori_loop(0, ts // ng, group, 0)

    _for_each_half(idx_hbm, (idx_a, idx_b), sems, process)
    coef_ref[...] = gate_ref[...] * jax.nn.gelu(act_scr[...])


def _peer_u_call(idx4, h2v, gate2, tab, ts, n_slabs, group):
    b, t, npair = idx4.shape
    tu = 2 * ts
    wide = gate2.shape[-1]
    return pl.pallas_call(
        functools.partial(_peer_u_kernel, group),
        grid=(b, t // tu),
        in_specs=[
            pl.BlockSpec(memory_space=pl.ANY),
            pl.BlockSpec((None, tu * SUBLANES, LANES), lambda bi, ti: (bi, ti, 0)),
            pl.BlockSpec((None, tu, wide), lambda bi, ti: (bi, ti, 0)),
            pl.BlockSpec(tab.shape, lambda bi, ti: (0, 0), pipeline_mode=pl.Buffered(1)),
        ],
        out_specs=pl.BlockSpec((None, tu, wide), lambda bi, ti: (bi, ti, 0)),
        out_shape=jax.ShapeDtypeStruct((b, t, wide), F32),
        scratch_shapes=[pltpu.VMEM((tu, wide), F32),
                        pltpu.SMEM((ts, npair), I32),
                        pltpu.SMEM((ts, npair), I32),
                        pltpu.SemaphoreType.DMA((2,))]
        + [pltpu.VMEM((npair * ROW_SLAB, LANES), I32)] * n_slabs,
        compiler_params=_params(("arbitrary", "arbitrary")),
        name="peer_u",
    )(idx4.reshape(b * t, npair), h2v, gate2, tab)


def _peer_v_kernel(final, ng, idx_hbm, coef_ref, x_ref, mod_ref, nf_ref, tab_ref, o_ref, cx_scr, out_scr,
                   idx_a, idx_b, sems, *slabs):
    ts, npair = idx_a.shape
    tv, d = x_ref.shape
    per = 2 * ROW_SLAB
    nk = npair * per
    ek = lax.broadcasted_iota(I32, (2 * npair, nk), 0)
    ej = lax.broadcasted_iota(I32, (2 * npair, nk), 1)
    expand = jnp.where((ej >> 3) * 2 == ek, 1.0, 0.0).astype(BF16)
    cx_scr[...] = _dot(coef_ref[...], expand)
    mrow = lax.broadcasted_iota(I32, (SUBLANES, nk), 0)
    jj = lax.broadcasted_iota(I32, (SUBLANES, nk), 1) & (per - 1)
    place = mrow == (jj >> 1) + ROW_SLAB * (jj & 1)

    def process(idx_ref, off):
        def group(g, carry):
            for i in range(ng):
                tl = g * ng + i
                if i % SUBLANES == 0:
                    cx8 = cx_scr[pl.ds(pl.multiple_of(off + tl, SUBLANES), SUBLANES), :]
                slab = slabs[i % len(slabs)]
                _gather_slabs(idx_ref, g, i, ng, tab_ref, slab)
                row = i % SUBLANES
                lhs = jnp.where(place, cx8[row:row + 1, :], 0.0).astype(BF16)
                out8 = jnp.dot(lhs, pltpu.bitcast(slab[...], BF16), preferred_element_type=F32)
                out_scr[pl.ds(pl.multiple_of((off + tl) * SUBLANES, SUBLANES), SUBLANES), :] = out8
            return carry

        lax.fori_loop(0, ts // ng, group, 0)

    _for_each_half(idx_hbm, (idx_a, idx_b), sems, process)
    g2 = mod_ref[:, 5 * d:6 * d]
    nchunk = d // LANES
    for c in range(nchunk):
        cs = slice(c * LANES, (c + 1) * LANES)
        o_ref[:, cs] = x_ref[:, cs] + g2[:, cs] * out_scr[pl.ds(c, tv, stride=nchunk), :]
    if final:
        o_ref[...] = _rms(o_ref[...], nf_ref[...])


def _peer_v_call(idx4, coef2, x1, mod, tab, norm_f, ts, final, n_slabs, group):
    b, t, npair = idx4.shape
    d = x1.shape[-1]
    tv = 2 * ts
    return pl.pallas_call(
        functools.partial(_peer_v_kernel, final, group),
        grid=(b, t // tv),
        in_specs=[
            pl.BlockSpec(memory_space=pl.ANY),
            pl.BlockSpec((None, tv, 2 * npair), lambda bi, ti: (bi, ti, 0)),
            pl.BlockSpec((None, tv, d), lambda bi, ti: (bi, ti, 0)),
            pl.BlockSpec((None, 1, mod.shape[-1]), lambda bi, ti: (bi, 0, 0)),
            pl.BlockSpec((1, d), lambda bi, ti: (0, 0)),
            pl.BlockSpec(tab.shape, lambda bi, ti: (0, 0), pipeline_mode=pl.Buffered(1)),
        ],
        out_specs=pl.BlockSpec((None, tv, d), lambda bi, ti: (bi, ti, 0)),
        out_shape=jax.ShapeDtypeStruct(x1.shape, F32),
        scratch_shapes=[pltpu.VMEM((tv, npair * 2 * ROW_SLAB), F32),
                        pltpu.VMEM((tv * SUBLANES, LANES), F32),
                        pltpu.SMEM((ts, npair), I32),
                        pltpu.SMEM((ts, npair), I32),
                        pltpu.SemaphoreType.DMA((2,))]
        + [pltpu.VMEM((npair * ROW_SLAB, LANES), I32)] * n_slabs,
        compiler_params=_params(("arbitrary", "arbitrary")),
        name="peer_v",
    )(idx4.reshape(b * t, npair), coef2, x1, mod, norm_f.reshape(1, d), tab)


def _block_diag(blocks):
    nl, g, r, c = blocks.shape
    eye = jnp.eye(g, dtype=blocks.dtype)
    return jnp.einsum("lgrc,gh->lgrhc", blocks, eye).reshape(nl, g * r, g * c)


def _chunk(t, pref):
    return pref if t % pref == 0 else t


def _run_group(x, mod, conv0, h0, s0r, s0i, w, tabs_u, tabs_v, norm_f):
    b, t, d = x.shape
    nl = mod.shape[0]
    tc = _chunk(t, 256)
    tb = _chunk(t, 128)
    ts = min(128, t // 2)
    convs, hs, srs, sis = [], [], [], []
    for l in range(nl):
        mod_l = mod[l].reshape(b, 1, mod.shape[-1])
        x1, nc, nh, nsr, nsi = _mixer_call(l, x, mod_l, conv0[l], h0[l], s0r[l], s0i[l], w, tc)
        convs.append(nc)
        hs.append(nh[:, 0])
        srs.append(nsr[:, 0])
        sis.append(nsi[:, 0])
        h2v, gate_t, eidx_t = _select_call(l, x1, mod_l, w["norm2"], w["peer_wq"], w["peer_keys"], tb)
        idx4 = jnp.transpose(eidx_t, (0, 2, 1)) * ROW_SLAB
        gate = jnp.transpose(gate_t, (0, 2, 1))
        gate2 = jnp.stack([gate, jnp.zeros_like(gate)], axis=-1).reshape(b, t, 2 * N_PAIRS)
        coef2 = _peer_u_call(idx4, h2v, gate2, tabs_u[l], ts, N_SLABS_U, min(ts, GROUP))
        x = _peer_v_call(idx4, coef2, x1, mod_l, tabs_v[l], norm_f, ts, l == nl - 1, N_SLABS_V, min(ts, GROUP))
    return x, jnp.stack(convs), jnp.stack(hs), jnp.stack(srs), jnp.stack(sis)


def kernel(x_prompt, x_sample, cache_conv, state_lru, state_s5_re, state_s5_im, c_prompt, c_sample,
           w_ada, b_ada, norm1, norm2, w_in, conv_w, conv_b, lru_wa, lru_ba, lru_wx, lru_bx, lru_lam,
           s5_a_re, s5_a_im, s5_log_dt, s5_b_re, s5_b_im, s5_c_re, s5_c_im, s5_d, s5_w_glu, s5_b_glu,
           gn_lru, gn_s5, w_out, peer_wq, peer_keys, peer_u, peer_v, norm_f):
    nl, d, _ = w_ada.shape
    bp = x_prompt.shape[0]
    bs = x_sample.shape[0]
    dl = state_lru.shape[-1]
    g, p = s5_a_re.shape[1:]
    ns = g * p

    mod = _ada_call(jnp.concatenate([c_prompt, c_sample], axis=0), w_ada, b_ada)

    bt_re = jnp.transpose(s5_b_re, (0, 1, 3, 2))
    bt_im = jnp.transpose(s5_b_im, (0, 1, 3, 2))
    ab_re, ab_im, bb_re, bb_im = _s5_disc_call(s5_a_re, s5_a_im, s5_log_dt, bt_re, bt_im)
    nblk = dl // LANES

    def blocks(a):
        per = a.shape[1] // nblk
        out = _block_diag(a.reshape((nl * nblk, per) + a.shape[2:]))
        return out.reshape((nl, nblk) + out.shape[1:]).astype(BF16)

    b_blk = jnp.concatenate([blocks(bb_re), blocks(bb_im)], axis=-1)
    c_re = blocks(jnp.transpose(s5_c_re, (0, 1, 3, 2)))
    c_im = blocks(jnp.transpose(s5_c_im, (0, 1, 3, 2)))
    w_gate = jnp.concatenate([blocks(lru_wa), blocks(lru_wx)], axis=-1)

    def vec(a):
        return a.reshape(nl, 1, a.shape[-1])

    w = {
        "norm1": vec(norm1), "w_in": w_in.astype(BF16), "conv_w": conv_w, "conv_b": vec(conv_b),
        "w_gate": w_gate, "b_gate": vec(jnp.concatenate([lru_ba, lru_bx], axis=-1)), "lam": vec(lru_lam),
        "ab_re": ab_re.reshape(nl, 1, ns), "ab_im": ab_im.reshape(nl, 1, ns),
        "b_blk": b_blk, "c_re": c_re, "c_im": c_im, "s5_d": vec(s5_d),
        "w_glu": s5_w_glu.astype(BF16), "b_glu": vec(s5_b_glu), "gn_lru": vec(gn_lru), "gn_s5": vec(gn_s5),
        "w_out": w_out.astype(BF16), "norm2": vec(norm2), "peer_wq": peer_wq.astype(BF16),
        "peer_keys": peer_keys.reshape(nl, 2 * PEER_HEADS, PEER_NKEYS, -1).astype(BF16),
    }
    tab_u = _pack_table(peer_u)
    tab_v = _pack_table(peer_v)

    zeros_conv = jnp.zeros((nl, bp, CONV_WIDTH - 1, dl), F32)
    zeros_lru = jnp.zeros((nl, bp, 1, dl), F32)
    zeros_s5 = jnp.zeros((nl, bp, 1, ns), F32)
    out_p = _run_group(x_prompt, mod[:, :bp], zeros_conv, zeros_lru, zeros_s5, zeros_s5,
                       w, tab_u, tab_v, norm_f)
    out_s = _run_group(x_sample, mod[:, bp:], cache_conv, state_lru.reshape(nl, bs, 1, dl),
                       state_s5_re.reshape(nl, bs, 1, ns), state_s5_im.reshape(nl, bs, 1, ns),
                       w, tab_u, tab_v, norm_f)

    def states(o, bsz):
        y, conv, hl, sr, si = o
        return y, conv, hl, sr.reshape(nl, bsz, g, p), si.reshape(nl, bsz, g, p)

    yp, cp, lp, rp, ip = states(out_p, bp)
    ys, cs, ls, rs, is_ = states(out_s, bs)
    return (yp, ys, cp, lp, rp, ip, cs, ls, rs, is_)
```

```python
import functools

import jax
import jax.numpy as jnp
from jax import lax
from jax.experimental import pallas as pl
from jax.experimental.pallas import tpu as pltpu

F32 = jnp.float32
BF16 = jnp.bfloat16
I32 = jnp.int32

EPS = 1e-6
LRU_C = 8.0
CONV_WIDTH = 4
PEER_HEADS = 8
PEER_NKEYS = 128
PEER_TOPK = 16
N_PAIRS = PEER_HEADS * PEER_TOPK

LANES = 128
SUBLANES = 8
VMEM_LIMIT = 56 * 1024 * 1024


def _rms(x, g):
    return x * lax.rsqrt(jnp.mean(x * x, axis=-1, keepdims=True) + EPS) * g


def _dot(a, b):
    return jnp.dot(a.astype(BF16), b, preferred_element_type=F32)


def _params(sem):
    return pltpu.CompilerParams(dimension_semantics=sem, vmem_limit_bytes=VMEM_LIMIT)


def _ada_kernel(c_ref, w_ref, b_ref, o_ref):
    c = c_ref[...]
    s = c * jax.nn.sigmoid(c)
    o_ref[...] = _dot(s, w_ref[...].astype(BF16)) + b_ref[...]


def _ada_call(c_all, w_ada, b_ada):
    nl, d, d6 = w_ada.shape
    nb = c_all.shape[0]
    bn = d
    return pl.pallas_call(
        _ada_kernel,
        grid=(nl, d6 // bn),
        in_specs=[
            pl.BlockSpec((nb, d), lambda l, j: (0, 0)),
            pl.BlockSpec((None, d, bn), lambda l, j: (l, 0, j)),
            pl.BlockSpec((None, 1, bn), lambda l, j: (l, 0, j)),
        ],
        out_specs=pl.BlockSpec((None, nb, bn), lambda l, j: (l, 0, j)),
        out_shape=jax.ShapeDtypeStruct((nl, nb, d6), F32),
        compiler_params=_params(("parallel", "parallel")),
        name="adaln_mod",
    )(c_all, w_ada, b_ada.reshape(nl, 1, d6))


def _s5_disc_kernel(are_ref, aim_ref, ldt_ref, bre_ref, bim_ref,
                    abre_ref, abim_ref, bbre_ref, bbim_ref):
    lr = are_ref[...]
    li = aim_ref[...]
    dt = jnp.exp(ldt_ref[...])
    mag = jnp.exp(lr * dt)
    ab_re = mag * jnp.cos(li * dt)
    ab_im = mag * jnp.sin(li * dt)
    den = lr * lr + li * li
    nr = ab_re - 1.0
    f_re = (nr * lr + ab_im * li) / den
    f_im = (ab_im * lr - nr * li) / den
    br = bre_ref[...]
    bi = bim_ref[...]
    abre_ref[...] = ab_re
    abim_ref[...] = ab_im
    bbre_ref[...] = f_re * br - f_im * bi
    bbim_ref[...] = f_re * bi + f_im * br


def _s5_disc_call(a_re, a_im, log_dt, bt_re, bt_im):
    nl, g, p = a_re.shape
    gi = bt_re.shape[2]
    spec_a = pl.BlockSpec((None, g, 1, p), lambda l: (l, 0, 0, 0))
    spec_b = pl.BlockSpec((None, g, gi, p), lambda l: (l, 0, 0, 0))
    return pl.pallas_call(
        _s5_disc_kernel,
        grid=(nl,),
        in_specs=[spec_a, spec_a, pl.BlockSpec((None, g, 1, 1), lambda l: (l, 0, 0, 0)), spec_b, spec_b],
        out_specs=[spec_a, spec_a, spec_b, spec_b],
        out_shape=[jax.ShapeDtypeStruct((nl, g, 1, p), F32)] * 2
        + [jax.ShapeDtypeStruct((nl, g, gi, p), F32)] * 2,
        compiler_params=_params(("parallel",)),
        name="s5_discretise",
    )(a_re.reshape(nl, g, 1, p), a_im.reshape(nl, g, 1, p), log_dt.reshape(nl, g, 1, 1), bt_re, bt_im)


SCAN_ROWS = 128


def _mixer_kernel(x_ref, mod_ref, n1_ref, win_ref, cw_ref, cb_ref, wg_ref, bg_ref, lam_ref,
                  are_ref, aim_ref, bblk_ref, cre_ref, cim_ref, d_ref, wglu_ref, bglu_ref,
                  gnl_ref, gns_ref, wout_ref, conv0_ref, h0_ref, s0r_ref, s0i_ref,
                  x1_ref, convo_ref, ho_ref, sro_ref, sio_ref,
                  xa_ext, hc, scr, sci, a_scr, b_scr, s_scr):
    tc, d = x_ref.shape
    dl = hc.shape[1]
    ns = scr.shape[1]
    nblk = wg_ref.shape[0]
    sw = ns // nblk
    tail = SUBLANES
    t = pl.program_id(1)

    @pl.when(t == 0)
    def _():
        xa_ext[0:tail, :] = jnp.zeros((tail, dl), F32)
        xa_ext[tail - (CONV_WIDTH - 1):tail, :] = conv0_ref[...]
        hc[...] = h0_ref[...]
        scr[...] = s0r_ref[...]
        sci[...] = s0i_ref[...]

    x = x_ref[...]
    mod = mod_ref[...]
    sh1, sc1, g1 = mod[:, 0:d], mod[:, d:2 * d], mod[:, 2 * d:3 * d]
    h = _rms(x, n1_ref[...]) * (1.0 + sc1) + sh1
    z = _dot(h, win_ref[...])
    xa = z[:, 0:dl]
    ga = z[:, dl:2 * dl]
    us = z[:, 2 * dl:]

    xa_ext[tail:tail + tc, :] = xa
    cw = cw_ref[...]
    xc = cb_ref[...] + cw[3:4, :] * xa
    for k in range(CONV_WIDTH - 1):
        back = CONV_WIDTH - 1 - k
        xc = xc + cw[k:k + 1, :] * xa_ext[tail - back:tail - back + tc, :]
    xa_ext[0:tail, :] = xa_ext[tc:tc + tail, :]
    convo_ref[...] = xa[tc - (CONV_WIDTH - 1):tc, :]

    nscan = min(tc, SCAN_ROWS)
    rows = lax.broadcasted_iota(I32, (nscan, LANES), 0)
    row0 = rows == 0

    xcb = xc.astype(BF16)
    gates = [jnp.dot(xcb[:, j * LANES:(j + 1) * LANES], wg_ref[j], preferred_element_type=F32)
             for j in range(nblk)]
    bg = bg_ref[...]
    r = jax.nn.sigmoid(jnp.concatenate([g[:, 0:LANES] for g in gates], axis=1) + bg[:, 0:dl])
    i = jax.nn.sigmoid(jnp.concatenate([g[:, LANES:] for g in gates], axis=1) + bg[:, dl:])
    log_a = -LRU_C * r * jax.nn.softplus(-lam_ref[...])
    a0 = jnp.exp(log_a)
    a_scr[...] = a0
    b_scr[...] = jnp.sqrt(-jnp.tanh(log_a) * (a0 * a0 + 1.0)) * (i * xc)

    def lru_block(j, carry):
        cs = pl.ds(pl.multiple_of(j * LANES, LANES), LANES)
        hprev = hc[:, cs]
        for r0 in range(0, tc, nscan):
            a = a_scr[r0:r0 + nscan, cs]
            bt = b_scr[r0:r0 + nscan, cs] + jnp.where(row0, a * hprev, 0.0)
            step = 1
            while step < nscan:
                m = rows >= step
                bt = jnp.where(m, a * pltpu.roll(bt, step, 0), 0.0) + bt
                if step * 2 < nscan:
                    a = jnp.where(m, a * pltpu.roll(a, step, 0), a)
                step *= 2
            b_scr[r0:r0 + nscan, cs] = bt
            hprev = bt[nscan - 1:nscan, :]
        return carry

    lax.fori_loop(0, dl // LANES, lru_block, 0)
    hl = b_scr[...]
    hc[...] = hl[tc - 1:tc, :]
    ho_ref[...] = hl[tc - 1:tc, :]
    y_lru = hl * jax.nn.gelu(ga)

    usb = us.astype(BF16)
    for j in range(nblk):
        bu = jnp.dot(usb[:, j * LANES:(j + 1) * LANES], bblk_ref[j], preferred_element_type=F32)
        s_scr[:, j * sw:(j + 1) * sw] = bu[:, 0:sw]
        s_scr[:, ns + j * sw:ns + (j + 1) * sw] = bu[:, sw:]

    def s5_block(j, carry):
        cr = pl.ds(pl.multiple_of(j * LANES, LANES), LANES)
        ci = pl.ds(pl.multiple_of(ns + j * LANES, LANES), LANES)
        c_r = scr[:, cr]
        c_i = sci[:, cr]
        for r0 in range(0, tc, nscan):
            pr = are_ref[:, cr]
            pi = aim_ref[:, cr]
            sr = s_scr[r0:r0 + nscan, cr] + jnp.where(row0, pr * c_r - pi * c_i, 0.0)
            si = s_scr[r0:r0 + nscan, ci] + jnp.where(row0, pr * c_i + pi * c_r, 0.0)
            step = 1
            while step < nscan:
                m = rows >= step
                qr = pltpu.roll(sr, step, 0)
                qi = pltpu.roll(si, step, 0)
                sr, si = (sr + jnp.where(m, pr * qr - pi * qi, 0.0),
                          si + jnp.where(m, pr * qi + pi * qr, 0.0))
                pr, pi = pr * pr - pi * pi, 2.0 * pr * pi
                step *= 2
            s_scr[r0:r0 + nscan, cr] = sr
            s_scr[r0:r0 + nscan, ci] = si
            c_r = sr[nscan - 1:nscan, :]
            c_i = si[nscan - 1:nscan, :]
        return carry

    lax.fori_loop(0, ns // LANES, s5_block, 0)
    sr = s_scr[:, 0:ns]
    si = s_scr[:, ns:]
    scr[...] = sr[tc - 1:tc, :]
    sci[...] = si[tc - 1:tc, :]
    sro_ref[...] = sr[tc - 1:tc, :]
    sio_ref[...] = si[tc - 1:tc, :]
    srb = sr.astype(BF16)
    sib = si.astype(BF16)
    ys = jnp.concatenate(
        [jnp.dot(srb[:, j * sw:(j + 1) * sw], cre_ref[j], preferred_element_type=F32)
         - jnp.dot(sib[:, j * sw:(j + 1) * sw], cim_ref[j], preferred_element_type=F32)
         for j in range(nblk)], axis=1) + d_ref[...] * us
    gy = jax.nn.gelu(ys)
    y_s5 = gy * jax.nn.sigmoid(_dot(gy, wglu_ref[...]) + bglu_ref[...])

    wout = wout_ref[...]
    mix = _dot(_rms(y_lru, gnl_ref[...]), wout[0:dl, :]) + _dot(_rms(y_s5, gns_ref[...]), wout[dl:, :])
    x1_ref[...] = x + g1 * mix


def _mixer_call(layer, x, mod, conv0, h0, s0r, s0i, w, tc):
    b, t, d = x.shape
    dl = h0.shape[-1]
    ns = s0r.shape[-1]

    def tok(shape):
        return pl.BlockSpec((None,) + shape, lambda bi, ti: (bi, ti, 0))

    def seq(shape):
        return pl.BlockSpec((None,) + shape, lambda bi, ti: (bi, 0, 0))

    def lay(arr):
        return pl.BlockSpec((None,) + arr.shape[1:], lambda bi, ti: (layer,) + (0,) * (arr.ndim - 1))

    names = ("norm1", "w_in", "conv_w", "conv_b", "w_gate", "b_gate", "lam", "ab_re", "ab_im",
             "b_blk", "c_re", "c_im", "s5_d", "w_glu", "b_glu", "gn_lru", "gn_s5", "w_out")
    ws = [w[n] for n in names]
    return pl.pallas_call(
        _mixer_kernel,
        grid=(b, t // tc),
        in_specs=[tok((tc, d)), seq((1, mod.shape[-1]))] + [lay(a) for a in ws]
        + [seq((CONV_WIDTH - 1, dl)), seq((1, dl)), seq((1, ns)), seq((1, ns))],
        out_specs=[tok((tc, d)), seq((CONV_WIDTH - 1, dl)), seq((1, dl)), seq((1, ns)), seq((1, ns))],
        out_shape=[jax.ShapeDtypeStruct((b, t, d), F32),
                   jax.ShapeDtypeStruct((b, CONV_WIDTH - 1, dl), F32),
                   jax.ShapeDtypeStruct((b, 1, dl), F32),
                   jax.ShapeDtypeStruct((b, 1, ns), F32),
                   jax.ShapeDtypeStruct((b, 1, ns), F32)],
        scratch_shapes=[pltpu.VMEM((tc + SUBLANES, dl), F32), pltpu.VMEM((1, dl), F32),
                        pltpu.VMEM((1, ns), F32), pltpu.VMEM((1, ns), F32),
                        pltpu.VMEM((tc, dl), F32), pltpu.VMEM((tc, dl), F32),
                        pltpu.VMEM((tc, 2 * ns), F32)],
        compiler_params=_params(("parallel", "arbitrary")),
        name="mixer",
    )(x, mod, *ws, conv0, h0, s0r, s0i)


def _topk_rows(s, k, payload=None):
    nrow = s.shape[0]
    rows = lax.broadcasted_iota(I32, s.shape, 0).astype(F32)
    vals, outs = [], []
    for _ in range(k):
        m = jnp.max(s, axis=0, keepdims=True)
        idx = jnp.min(jnp.where(s == m, rows, float(nrow)), axis=0, keepdims=True)
        hit = rows == idx
        vals.append(m)
        if payload is None:
            outs.append(idx)
        else:
            outs.append(jnp.max(jnp.where(hit, payload, -1.0), axis=0, keepdims=True))
        s = jnp.where(hit, -jnp.inf, s)
    return jnp.concatenate(vals, axis=0), jnp.concatenate(outs, axis=0)


def _select_kernel(x_ref, mod_ref, n2_ref, wq_ref, keys_ref, h2_ref, gate_ref, eidx_ref):
    tb, d = x_ref.shape
    x = x_ref[...]
    mod = mod_ref[...]
    sh2, sc2 = mod[:, 3 * d:4 * d], mod[:, 4 * d:5 * d]
    h2 = _rms(x, n2_ref[...]) * (1.0 + sc2) + sh2
    nchunk = d // LANES
    for c in range(nchunk):
        h2_ref[pl.ds(c, tb, stride=nchunk), :] = h2[:, c * LANES:(c + 1) * LANES]
    q = _dot(h2, wq_ref[...])
    counts = [PEER_TOPK // (a + 1) for a in range(PEER_TOPK)]
    npad = -sum(counts) % SUBLANES
    dk = keys_ref.shape[2]
    for hd in range(PEER_HEADS):
        vs, ids = [], []
        for c in range(2):
            j = 2 * hd + c
            qhc = q[:, j * dk:(j + 1) * dk].astype(BF16)
            st = lax.dot_general(keys_ref[j], qhc, (((1,), (1,)), ((), ())),
                                 preferred_element_type=F32)
            v, ix = _topk_rows(st, PEER_TOPK)
            vs.append(v)
            ids.append(ix)
        cand = jnp.concatenate(
            [vs[0][a:a + 1, :] + vs[1][0:counts[a], :] for a in range(PEER_TOPK)]
            + [jnp.full((npad, tb), -jnp.inf, F32)], axis=0)
        cidx = jnp.concatenate(
            [ids[0][a:a + 1, :] * float(PEER_NKEYS) + ids[1][0:counts[a], :] for a in range(PEER_TOPK)]
            + [jnp.zeros((npad, tb), F32)], axis=0)
        top_s, eidx = _topk_rows(cand, PEER_TOPK, payload=cidx)
        e = jnp.exp(top_s - jnp.max(top_s, axis=0, keepdims=True))
        gate = e / jnp.sum(e, axis=0, keepdims=True)
        gate_ref[hd * PEER_TOPK:(hd + 1) * PEER_TOPK, :] = gate
        eidx_ref[hd * PEER_TOPK:(hd + 1) * PEER_TOPK, :] = eidx.astype(I32)


def _select_call(layer, x1, mod, norm2, wq, keys, tb):
    b, t, d = x1.shape
    return pl.pallas_call(
        _select_kernel,
        grid=(b, t // tb),
        in_specs=[
            pl.BlockSpec((None, tb, d), lambda bi, ti: (bi, ti, 0)),
            pl.BlockSpec((None, 1, mod.shape[-1]), lambda bi, ti: (bi, 0, 0)),
            pl.BlockSpec((None, 1, d), lambda bi, ti: (layer, 0, 0)),
            pl.BlockSpec((None,) + wq.shape[1:], lambda bi, ti: (layer, 0, 0)),
            pl.BlockSpec((None,) + keys.shape[1:], lambda bi, ti: (layer, 0, 0, 0)),
        ],
        out_specs=[
            pl.BlockSpec((None, tb * (d // LANES), LANES), lambda bi, ti: (bi, ti, 0)),
            pl.BlockSpec((None, N_PAIRS, tb), lambda bi, ti: (bi, 0, ti)),
            pl.BlockSpec((None, N_PAIRS, tb), lambda bi, ti: (bi, 0, ti)),
        ],
        out_shape=[jax.ShapeDtypeStruct((b, t * (d // LANES), LANES), F32),
                   jax.ShapeDtypeStruct((b, N_PAIRS, t), F32),
                   jax.ShapeDtypeStruct((b, N_PAIRS, t), I32)],
        compiler_params=_params(("parallel", "parallel")),
        name="peer_select",
    )(x1, mod, norm2, wq, keys)


ROW_SLAB = 4
GROUP = 4 * SUBLANES
N_SLABS_U = 2
N_SLABS_V = 4


def _pack_table(tab):
    nl, ne, d = tab.shape
    bits = lax.bitcast_convert_type(tab.astype(BF16), jnp.uint16).astype(jnp.uint32)
    words = bits[:, :, :d // 2] | (bits[:, :, d // 2:] << 16)
    return lax.bitcast_convert_type(words, I32).reshape(nl, ne * ROW_SLAB, LANES)


def _gather_slabs(idx_ref, g, i, ng, tab_ref, slab_ref):
    span = idx_ref.shape[0] - ng + 1
    for k in range(idx_ref.shape[1]):
        r = pl.multiple_of(idx_ref.at[pl.ds(i, span), pl.ds(k, 1)][g * ng, 0], ROW_SLAB)
        slab_ref[k * ROW_SLAB:(k + 1) * ROW_SLAB, :] = tab_ref[pl.ds(r, ROW_SLAB), :]


def _for_each_half(idx_hbm, bufs, sems, process):
    ts = bufs[0].shape[0]
    step = pl.program_id(0) * pl.num_programs(1) + pl.program_id(1)
    nstep = pl.num_programs(0) * pl.num_programs(1)

    def copy(blk, which):
        return pltpu.make_async_copy(idx_hbm.at[pl.ds(blk * ts, ts), :], bufs[which], sems.at[which])

    @pl.when(step == 0)
    def _():
        copy(0, 0).start()

    copy(2 * step + 1, 1).start()
    copy(2 * step, 0).wait()
    process(bufs[0], 0)

    @pl.when(step + 1 < nstep)
    def _():
        copy(2 * step + 2, 0).start()

    copy(2 * step + 1, 1).wait()
    process(bufs[1], ts)


def _peer_u_kernel(ng, idx_hbm, x_ref, gate_ref, tab_ref, coef_ref, act_scr, idx_a, idx_b, sems, *slabs):
    ts, npair = idx_a.shape
    wide = 2 * npair
    kcat = ROW_SLAB * LANES
    rows2 = lax.broadcasted_iota(I32, (SUBLANES, wide), 0)
    lanes2 = lax.broadcasted_iota(I32, (SUBLANES, wide), 1)
    keep = (rows2 < ROW_SLAB) == ((lanes2 & 1) == 0)
    rowk = lax.broadcasted_iota(I32, (SUBLANES, kcat), 0)
    blk = lax.broadcasted_iota(I32, (SUBLANES, kcat), 1) >> 7
    xmask = (rowk == blk) | (rowk == blk + ROW_SLAB)

    def process(idx_ref, off):
        def group(g, carry):
            accs = [jnp.zeros((SUBLANES, wide), F32) for _ in range(ng // SUBLANES)]
            for i in range(ng):
                tl = g * ng + i
                slab = slabs[i % len(slabs)]
                _gather_slabs(idx_ref, g, i, ng, tab_ref, slab)
                x8 = x_ref[pl.ds(pl.multiple_of((off + tl) * SUBLANES, SUBLANES), SUBLANES), :]
                lhs = jnp.where(xmask, jnp.concatenate([x8] * ROW_SLAB, axis=1), 0.0).astype(BF16)
                rhs = jnp.concatenate(
                    [pltpu.bitcast(slab[pl.ds(c, npair, stride=ROW_SLAB), :], BF16) for c in range(ROW_SLAB)],
                    axis=1)
                out = lax.dot_general(lhs, rhs, (((1,), (1,)), ((), ())), preferred_element_type=F32)
                z = jnp.sum(jnp.where(keep, out, 0.0), axis=0, keepdims=True)
                accs[i // SUBLANES] = jnp.where(rows2 == i % SUBLANES, z, accs[i // SUBLANES])
            for j, acc in enumerate(accs):
                act = acc + pltpu.roll(acc, wide - 1, 1)
                r0 = pl.multiple_of(off + g * ng + j * SUBLANES, SUBLANES)
                act_scr[pl.ds(r0, SUBLANES), :] = act
            return carry

        lax.fori_loop(0, ts // ng, group, 0)

    _for_each_half(idx_hbm, (idx_a, idx_b), sems, process)
    coef_ref[...] = gate_ref[...] * jax.nn.gelu(act_scr[...])


def _peer_u_call(idx4, h2v, gate2, tab, ts, n_slabs, group):
    b, t, npair = idx4.shape
    tu = 2 * ts
    wide = gate2.shape[-1]
    return pl.pallas_call(
        functools.partial(_peer_u_kernel, group),
        grid=(b, t // tu),
        in_specs=[
            pl.BlockSpec(memory_space=pl.ANY),
            pl.BlockSpec((None, tu * SUBLANES, LANES), lambda bi, ti: (bi, ti, 0)),
            pl.BlockSpec((None, tu, wide), lambda bi, ti: (bi, ti, 0)),
            pl.BlockSpec(tab.shape, lambda bi, ti: (0, 0), pipeline_mode=pl.Buffered(1)),
        ],
        out_specs=pl.BlockSpec((None, tu, wide), lambda bi, ti: (bi, ti, 0)),
        out_shape=jax.ShapeDtypeStruct((b, t, wide), F32),
        scratch_shapes=[pltpu.VMEM((tu, wide), F32),
                        pltpu.SMEM((ts, npair), I32),
                        pltpu.SMEM((ts, npair), I32),
                        pltpu.SemaphoreType.DMA((2,))]
        + [pltpu.VMEM((npair * ROW_SLAB, LANES), I32)] * n_slabs,
        compiler_params=_params(("arbitrary", "arbitrary")),
        name="peer_u",
    )(idx4.reshape(b * t, npair), h2v, gate2, tab)


def _peer_v_kernel(final, ng, idx_hbm, coef_ref, x_ref, mod_ref, nf_ref, tab_ref, o_ref, cx_scr, out_scr,
                   idx_a, idx_b, sems, *slabs):
    ts, npair = idx_a.shape
    tv, d = x_ref.shape
    per = 2 * ROW_SLAB
    nk = npair * per
    half = nk // 2
    ek = lax.broadcasted_iota(I32, (2 * npair, nk), 0)
    ej = lax.broadcasted_iota(I32, (2 * npair, nk), 1)
    expand = jnp.where((ej >> 3) * 2 == ek, 1.0, 0.0).astype(BF16)
    cx_scr[...] = _dot(coef_ref[...], expand)
    mrow = lax.broadcasted_iota(I32, (SUBLANES, nk), 0)
    jj = lax.broadcasted_iota(I32, (SUBLANES, nk), 1) & (per - 1)
    place = mrow == (jj >> 1) + ROW_SLAB * (jj & 1)

    def process(idx_ref, off):
        def group(g, carry):
            for i in range(ng):
                tl = g * ng + i
                if i % SUBLANES == 0:
                    cx8 = cx_scr[pl.ds(pl.multiple_of(off + tl, SUBLANES), SUBLANES), :]
                slab = slabs[i % len(slabs)]
                _gather_slabs(idx_ref, g, i, ng, tab_ref, slab)
                row = i % SUBLANES
                coefs = jnp.where(place, cx8[row:row + 1, :], 0.0)
                lhs = jnp.concatenate([coefs[:, 0:half], coefs[:, half:]], axis=0).astype(BF16)
                w = pltpu.bitcast(slab[...], BF16)
                out2 = jnp.dot(lhs, jnp.concatenate([w[0:half, :], w[half:, :]], axis=1),
                               preferred_element_type=F32)
                out8 = out2[0:SUBLANES, 0:LANES] + out2[SUBLANES:, LANES:]
                out_scr[pl.ds(pl.multiple_of((off + tl) * SUBLANES, SUBLANES), SUBLANES), :] = out8
            return carry

        lax.fori_loop(0, ts // ng, group, 0)

    _for_each_half(idx_hbm, (idx_a, idx_b), sems, process)
    g2 = mod_ref[:, 5 * d:6 * d]
    nchunk = d // LANES
    for c in range(nchunk):
        cs = slice(c * LANES, (c + 1) * LANES)
        o_ref[:, cs] = x_ref[:, cs] + g2[:, cs] * out_scr[pl.ds(c, tv, stride=nchunk), :]
    if final:
        o_ref[...] = _rms(o_ref[...], nf_ref[...])


def _peer_v_call(idx4, coef2, x1, mod, tab, norm_f, ts, final, n_slabs, group):
    b, t, npair = idx4.shape
    d = x1.shape[-1]
    tv = 2 * ts
    return pl.pallas_call(
        functools.partial(_peer_v_kernel, final, group),
        grid=(b, t // tv),
        in_specs=[
            pl.BlockSpec(memory_space=pl.ANY),
            pl.BlockSpec((None, tv, 2 * npair), lambda bi, ti: (bi, ti, 0)),
            pl.BlockSpec((None, tv, d), lambda bi, ti: (bi, ti, 0)),
            pl.BlockSpec((None, 1, mod.shape[-1]), lambda bi, ti: (bi, 0, 0)),
            pl.BlockSpec((1, d), lambda bi, ti: (0, 0)),
            pl.BlockSpec(tab.shape, lambda bi, ti: (0, 0), pipeline_mode=pl.Buffered(1)),
        ],
        out_specs=pl.BlockSpec((None, tv, d), lambda bi, ti: (bi, ti, 0)),
        out_shape=jax.ShapeDtypeStruct(x1.shape, F32),
        scratch_shapes=[pltpu.VMEM((tv, npair * 2 * ROW_SLAB), F32),
                        pltpu.VMEM((tv * SUBLANES, LANES), F32),
                        pltpu.SMEM((ts, npair), I32),
                        pltpu.SMEM((ts, npair), I32),
                        pltpu.SemaphoreType.DMA((2,))]
        + [pltpu.VMEM((npair * ROW_SLAB, LANES), I32)] * n_slabs,
        compiler_params=_params(("arbitrary", "arbitrary")),
        name="peer_v",
    )(idx4.reshape(b * t, npair), coef2, x1, mod, norm_f.reshape(1, d), tab)


def _block_diag(blocks):
    nl, g, r, c = blocks.shape
    eye = jnp.eye(g, dtype=blocks.dtype)
    return jnp.einsum("lgrc,gh->lgrhc", blocks, eye).reshape(nl, g * r, g * c)


def _chunk(t, pref):
    return pref if t % pref == 0 else t


def _run_group(x, mod, conv0, h0, s0r, s0i, w, tabs_u, tabs_v, norm_f):
    b, t, d = x.shape
    nl = mod.shape[0]
    tc = _chunk(t, 256)
    tb = _chunk(t, 128)
    ts = min(128, t // 2)
    convs, hs, srs, sis = [], [], [], []
    for l in range(nl):
        mod_l = mod[l].reshape(b, 1, mod.shape[-1])
        x1, nc, nh, nsr, nsi = _mixer_call(l, x, mod_l, conv0[l], h0[l], s0r[l], s0i[l], w, tc)
        convs.append(nc)
        hs.append(nh[:, 0])
        srs.append(nsr[:, 0])
        sis.append(nsi[:, 0])
        h2v, gate_t, eidx_t = _select_call(l, x1, mod_l, w["norm2"], w["peer_wq"], w["peer_keys"], tb)
        idx4 = jnp.transpose(eidx_t, (0, 2, 1)) * ROW_SLAB
        gate = jnp.transpose(gate_t, (0, 2, 1))
        gate2 = jnp.stack([gate, jnp.zeros_like(gate)], axis=-1).reshape(b, t, 2 * N_PAIRS)
        coef2 = _peer_u_call(idx4, h2v, gate2, tabs_u[l], ts, N_SLABS_U, min(ts, GROUP))
        x = _peer_v_call(idx4, coef2, x1, mod_l, tabs_v[l], norm_f, ts, l == nl - 1, N_SLABS_V, min(ts, GROUP))
    return x, jnp.stack(convs), jnp.stack(hs), jnp.stack(srs), jnp.stack(sis)


def kernel(x_prompt, x_sample, cache_conv, state_lru, state_s5_re, state_s5_im, c_prompt, c_sample,
           w_ada, b_ada, norm1, norm2, w_in, conv_w, conv_b, lru_wa, lru_ba, lru_wx, lru_bx, lru_lam,
           s5_a_re, s5_a_im, s5_log_dt, s5_b_re, s5_b_im, s5_c_re, s5_c_im, s5_d, s5_w_glu, s5_b_glu,
           gn_lru, gn_s5, w_out, peer_wq, peer_keys, peer_u, peer_v, norm_f):
    nl, d, _ = w_ada.shape
    bp = x_prompt.shape[0]
    bs = x_sample.shape[0]
    dl = state_lru.shape[-1]
    g, p = s5_a_re.shape[1:]
    ns = g * p

    mod = _ada_call(jnp.concatenate([c_prompt, c_sample], axis=0), w_ada, b_ada)

    bt_re = jnp.transpose(s5_b_re, (0, 1, 3, 2))
    bt_im = jnp.transpose(s5_b_im, (0, 1, 3, 2))
    ab_re, ab_im, bb_re, bb_im = _s5_disc_call(s5_a_re, s5_a_im, s5_log_dt, bt_re, bt_im)
    nblk = dl // LANES

    def blocks(a):
        per = a.shape[1] // nblk
        out = _block_diag(a.reshape((nl * nblk, per) + a.shape[2:]))
        return out.reshape((nl, nblk) + out.shape[1:]).astype(BF16)

    b_blk = jnp.concatenate([blocks(bb_re), blocks(bb_im)], axis=-1)
    c_re = blocks(jnp.transpose(s5_c_re, (0, 1, 3, 2)))
    c_im = blocks(jnp.transpose(s5_c_im, (0, 1, 3, 2)))
    w_gate = jnp.concatenate([blocks(lru_wa), blocks(lru_wx)], axis=-1)

    def vec(a):
        return a.reshape(nl, 1, a.shape[-1])

    w = {
        "norm1": vec(norm1), "w_in": w_in.astype(BF16), "conv_w": conv_w, "conv_b": vec(conv_b),
        "w_gate": w_gate, "b_gate": vec(jnp.concatenate([lru_ba, lru_bx], axis=-1)), "lam": vec(lru_lam),
        "ab_re": ab_re.reshape(nl, 1, ns), "ab_im": ab_im.reshape(nl, 1, ns),
        "b_blk": b_blk, "c_re": c_re, "c_im": c_im, "s5_d": vec(s5_d),
        "w_glu": s5_w_glu.astype(BF16), "b_glu": vec(s5_b_glu), "gn_lru": vec(gn_lru), "gn_s5": vec(gn_s5),
        "w_out": w_out.astype(BF16), "norm2": vec(norm2), "peer_wq": peer_wq.astype(BF16),
        "peer_keys": peer_keys.reshape(nl, 2 * PEER_HEADS, PEER_NKEYS, -1).astype(BF16),
    }
    tab_u = _pack_table(peer_u)
    tab_v = _pack_table(peer_v)

    zeros_conv = jnp.zeros((nl, bp, CONV_WIDTH - 1, dl), F32)
    zeros_lru = jnp.zeros((nl, bp, 1, dl), F32)
    zeros_s5 = jnp.zeros((nl, bp, 1, ns), F32)
    out_p = _run_group(x_prompt, mod[:, :bp], zeros_conv, zeros_lru, zeros_s5, zeros_s5,
                       w, tab_u, tab_v, norm_f)
    out_s = _run_group(x_sample, mod[:, bp:], cache_conv, state_lru.reshape(nl, bs, 1, dl),
                       state_s5_re.reshape(nl, bs, 1, ns), state_s5_im.reshape(nl, bs, 1, ns),
                       w, tab_u, tab_v, norm_f)

    def states(o, bsz):
        y, conv, hl, sr, si = o
        return y, conv, hl, sr.reshape(nl, bsz, g, p), si.reshape(nl, bsz, g, p)

    yp, cp, lp, rp, ip = states(out_p, bp)
    ys, cs, ls, rs, is_ = states(out_s, bs)
    return (yp, ys, cp, lp, rp, ip, cs, ls, rs, is_)
```
